```python
import math
import jax
import jax.numpy as jnp
from jax import lax
import numpy as np

D_MODEL = 4096
BATCH = 4
SEQ = 2048
DEPTH = 2
DEC_BATCH = 8
DEC_SEQ = 4
PAST_LEN = 16384
PAGE_SIZE = 128

POOL_WINDOWS = (2, 4, 8, 16)
POOL_GROUPS = len(POOL_WINDOWS)
POOL_WIDTH = 3 * D_MODEL // 8
POOL_GC = POOL_WIDTH // POOL_GROUPS
POOL_KEEP = max(POOL_WINDOWS) - 1
DILATED_GROUPS = ((128, 1), (512, 4), (2048, 16))
N_DIL_GROUPS = len(DILATED_GROUPS)
HEADS_PER_GROUP = 4
HEAD_DIM = 128
N_ATTN_HEADS = HEADS_PER_GROUP * N_DIL_GROUPS
ATTN_WIDTH = N_ATTN_HEADS * HEAD_DIM
ATTN_OUT = HEADS_PER_GROUP * HEAD_DIM
ATTN_QBLOCK = 128
NUM_BUCKETS = 32
MAX_DISTANCE = 2048
GLA_HEADS = 4
GLA_DV = 3 * D_MODEL // 8
GLA_DK = GLA_DV // 2
GLA_HK = GLA_DK // GLA_HEADS
GLA_HV = GLA_DV // GLA_HEADS
GLA_RANK = 16
GLA_TAU = 16.0
GLA_CHUNK = 64
D_FF = ((8 * D_MODEL // 3 + 255) // 256) * 256
ALPHA = (2 * DEPTH) ** 0.25
BETA = (8 * DEPTH) ** -0.25
LN_EPS = 1e-5
RMS_EPS = 1e-6
NEG_INF = -1e30
IN_SECTIONS = (POOL_WIDTH, ATTN_WIDTH, ATTN_WIDTH, ATTN_WIDTH, GLA_DK, GLA_DK, GLA_DV, GLA_RANK, GLA_DV, 3 * D_MODEL)
IN_SCALES = (BETA, 1.0, 1.0, BETA, 1.0, 1.0, BETA, 1.0, 1.0, 1.0)
N_IN = sum(IN_SECTIONS)

kernel_name = "hybrid_pool_dilated_gla_decoder_step"


def _layer_norm(x, g, b):
    xf = x.astype(jnp.float32)
    mu = jnp.mean(xf, axis=-1, keepdims=True)
    var = jnp.mean(jnp.square(xf - mu), axis=-1, keepdims=True)
    return ((xf - mu) * lax.rsqrt(var + LN_EPS) * g.astype(jnp.float32) + b.astype(jnp.float32)).astype(x.dtype)


def _swiglu(x, wg, wu, wd):
    return (jax.nn.silu(x @ wg) * (x @ wu)) @ wd


def _split_points():
    pts, acc = [], 0
    for w in IN_SECTIONS[:-1]:
        acc += w
        pts.append(acc)
    return pts


def _pool_mixer(u_ext, p0, T, w_grp, scale):
    B, L, C = u_ext.shape
    P = L - T
    uf = u_ext.astype(jnp.float32)
    cs = jnp.concatenate([jnp.zeros((B, 1, C), jnp.float32), lax.cumsum(uf, axis=1)], axis=1)
    pos = p0 + jnp.arange(T)
    parts = []
    for gi, w in enumerate(POOL_WINDOWS):
        sl = slice(gi * POOL_GC, (gi + 1) * POOL_GC)
        win = cs[:, P + 1:P + 1 + T, sl] - cs[:, P + 1 - w:P + 1 - w + T, sl]
        cnt = jnp.minimum(pos + 1, w).astype(jnp.float32)
        parts.append(win / cnt[None, :, None] - uf[:, P:, sl])
    d = jnp.stack(parts, axis=2)
    y = jnp.einsum('btgc,gce->btge', d, w_grp.astype(jnp.float32)).reshape(B, T, C)
    return (y * scale.astype(jnp.float32)).astype(u_ext.dtype)


def _t5_bias(rel_bias, g):
    W, d = DILATED_GROUPS[g]
    dist = d * jnp.arange(W // d + 1)
    max_exact = NUM_BUCKETS // 2
    df = jnp.maximum(dist, 1).astype(jnp.float32)
    large = max_exact + (jnp.log(df / max_exact) / math.log(MAX_DISTANCE / max_exact)
                         * (NUM_BUCKETS - max_exact)).astype(jnp.int32)
    bucket = jnp.where(dist < max_exact, dist, jnp.minimum(large, NUM_BUCKETS - 1))
    return rel_bias[bucket][:, g * HEADS_PER_GROUP:(g + 1) * HEADS_PER_GROUP].T.astype(jnp.float32)


def _dilated_attention(q, kv_exts, p0, rel_bias):
    B, T = q.shape[:2]
    QB = math.gcd(T, ATTN_QBLOCK)
    scale = HEAD_DIM ** -0.5
    biases = [_t5_bias(rel_bias, g) for g in range(N_DIL_GROUPS)]

    def block(i0):
        ms, ls, os_ = [], [], []
        for g, (W, d) in enumerate(DILATED_GROUPS):
            kv = kv_exts[g]
            L = kv.shape[1]
            P = L - T
            nk = W // d + 1
            qi = lax.dynamic_slice_in_dim(q[:, :, g], i0, QB, axis=1).astype(jnp.float32)
            off = jnp.arange(QB)[:, None] - d * jnp.arange(nk)[None, :]
            idx = jnp.clip(P + i0 + off, 0, L - 1)
            valid = (p0 + i0 + off) >= 0
            kvg = jnp.take(kv, idx, axis=1, mode='clip').astype(jnp.float32)
            s = jnp.einsum('bqhd,bqkhd->bhqk', qi, kvg[:, :, :, 0]) * scale + biases[g][None, :, None, :]
            s = jnp.where(valid[None, None], s, NEG_INF)
            m = jnp.max(s, axis=-1)
            p = jnp.exp(s - m[..., None])
            ms.append(m)
            ls.append(jnp.sum(p, axis=-1))
            os_.append(jnp.einsum('bhqk,bqkhd->bqhd', p, kvg[:, :, :, 1]))
        ms = jnp.stack(ms)
        ls = jnp.stack(ls)
        os_ = jnp.stack(os_)
        c = jnp.exp(ms - jnp.max(ms, axis=0))
        den = jnp.sum(ls * c, axis=0)
        num = jnp.sum(os_ * jnp.swapaxes(c, -1, -2)[..., None], axis=0)
        return (num / jnp.swapaxes(den, -1, -2)[..., None]).astype(q.dtype)

    out = lax.map(block, jnp.arange(T // QB) * QB)
    return out.transpose(1, 0, 2, 3, 4).reshape(B, T, ATTN_OUT)


def _gla(q, k, v, glog, S0):
    B, T, H, _ = q.shape
    C = math.gcd(T, GLA_CHUNK)
    nc = T // C

    def chunks(a):
        return a.astype(jnp.float32).reshape(B, nc, C, H, a.shape[-1]).transpose(1, 0, 3, 2, 4)

    qc = chunks(q) * (GLA_HK ** -0.5)
    kc, vc, gc = chunks(k), chunks(v), chunks(glog)
    mask = jnp.tril(jnp.ones((C, C), dtype=bool))

    def step(S, inp):
        qi, ki, vi, gi = inp
        G = lax.cumsum(gi, axis=2)
        Gl = G[:, :, -1:, :]
        qg = qi * jnp.exp(G)
        A = jnp.einsum('bhik,bhjk->bhij', qg, ki * jnp.exp(-G))
        A = jnp.where(mask, A, 0.0)
        o = jnp.einsum('bhik,bhkv->bhiv', qg, S) + jnp.einsum('bhij,bhjv->bhiv', A, vi)
        S = jnp.exp(Gl[:, :, 0, :, None]) * S + jnp.einsum('bhjk,bhjv->bhkv', ki * jnp.exp(Gl - G), vi)
        return S, o

    S, o = lax.scan(step, S0.astype(jnp.float32), (qc, kc, vc, gc))
    return o.transpose(1, 0, 3, 2, 4).reshape(B, T, H, GLA_HV), S


def _layer(x, p0, pool_buf, kv_bufs, gla_S, rel_bias, lw):
    B, T, _ = x.shape
    dt = x.dtype
    x = _layer_norm(ALPHA * x + 0.5 * _swiglu(x, lw['ffn_w_gate'][0], lw['ffn_w_up'][0], lw['ffn_w_down'][0]),
                    lw['ln_gain'][0], lw['ln_bias'][0])
    h = x @ lw['w_in']
    u, aq, ak, av, gq, gk, gv, glow, gr, gates = jnp.split(h, _split_points(), axis=-1)
    u_ext = jnp.concatenate([pool_buf.astype(dt), u], axis=1)
    y_pool = _pool_mixer(u_ext, p0, T, lw['pool_w'], lw['pool_scale'])
    new_pool = u_ext[:, -POOL_KEEP:]
    shp = (B, T, N_DIL_GROUPS, HEADS_PER_GROUP, HEAD_DIM)
    aq = aq.reshape(shp)
    akv = jnp.stack([ak.reshape(shp), av.reshape(shp)], axis=3)
    kv_exts = [jnp.concatenate([kv_bufs[g].astype(dt), akv[:, :, g]], axis=1) for g in range(N_DIL_GROUPS)]
    y_attn = _dilated_attention(aq, kv_exts, p0, rel_bias)
    new_kv = [e[:, -min(W, p0 + T):] for e, (W, _) in zip(kv_exts, DILATED_GROUPS)]
    z = (glow @ lw['gla_w_gate'] + lw['gla_b_gate']).astype(jnp.float32)
    glog = jax.nn.log_sigmoid(z) / GLA_TAU
    o, new_S = _gla(gq.reshape(B, T, GLA_HEADS, GLA_HK), gk.reshape(B, T, GLA_HEADS, GLA_HK),
                    gv.reshape(B, T, GLA_HEADS, GLA_HV), glog.reshape(B, T, GLA_HEADS, GLA_HK), gla_S)
    o = o * lax.rsqrt(jnp.mean(o * o, axis=-1, keepdims=True) + RMS_EPS) * lw['gla_norm'].astype(jnp.float32)
    y_gla = (o.reshape(B, T, GLA_DV) * jax.nn.silu(gr.astype(jnp.float32))).astype(dt)
    g_pool, g_attn, g_gla = jnp.split(jax.nn.sigmoid(gates), 3, axis=-1)
    merged = (g_pool * (y_pool @ lw['w_br_pool']) + g_attn * (y_attn @ lw['w_br_attn'])
              + g_gla * (y_gla @ lw['w_br_gla']))
    x = _layer_norm(ALPHA * x + merged @ lw['w_out'], lw['ln_gain'][1], lw['ln_bias'][1])
    x = _layer_norm(ALPHA * x + 0.5 * _swiglu(x, lw['ffn_w_gate'][1], lw['ffn_w_up'][1], lw['ffn_w_down'][1]),
                    lw['ln_gain'][2], lw['ln_bias'][2])
    return x, new_pool, new_kv, new_S.astype(gla_S.dtype)


def setup_inputs(seed: int = 0) -> dict:
    key = jax.random.key(seed)
    ks = list(jax.random.split(key, 32))
    f32 = jnp.float32

    def nrm(i, shape, scale):
        return jax.random.normal(ks[i], shape, f32) * scale

    col_scale = jnp.concatenate([jnp.full((w,), s, f32) for w, s in zip(IN_SECTIONS, IN_SCALES)])
    kv_shape = lambda W: (DEPTH, DEC_BATCH, min(W, PAST_LEN), 2, HEADS_PER_GROUP, HEAD_DIM)
    return {
        'x_prompt': nrm(0, (BATCH, SEQ, D_MODEL), 1.0),
        'x_sample': nrm(1, (DEC_BATCH, DEC_SEQ, D_MODEL), 1.0),
        'cache_pool': nrm(2, (DEPTH, DEC_BATCH, POOL_KEEP, POOL_WIDTH), 1.0),
        'cache_kv_w128': nrm(3, kv_shape(DILATED_GROUPS[0][0]), 1.0),
        'cache_kv_w512': nrm(4, kv_shape(DILATED_GROUPS[1][0]), 1.0),
        'cache_kv_w2048': nrm(5, kv_shape(DILATED_GROUPS[2][0]), 1.0),
        'state_gla': nrm(6, (DEPTH, DEC_BATCH, GLA_HEADS, GLA_HK, GLA_HV), 0.5),
        'rel_bias': nrm(7, (NUM_BUCKETS, N_ATTN_HEADS), 0.2),
        'ln_gain': 1.0 + nrm(8, (DEPTH, 3, D_MODEL), 0.05),
        'ln_bias': nrm(9, (DEPTH, 3, D_MODEL), 0.02),
        'ffn_w_gate': nrm(10, (DEPTH, 2, D_MODEL, D_FF), BETA * D_MODEL ** -0.5),
        'ffn_w_up': nrm(11, (DEPTH, 2, D_MODEL, D_FF), BETA * D_MODEL ** -0.5),
        'ffn_w_down': nrm(12, (DEPTH, 2, D_FF, D_MODEL), BETA * D_FF ** -0.5),
        'w_in': nrm(13, (DEPTH, D_MODEL, N_IN), D_MODEL ** -0.5) * col_scale,
        'pool_w': nrm(14, (DEPTH, POOL_GROUPS, POOL_GC, POOL_GC), POOL_GC ** -0.5),
        'pool_scale': 1.0 + nrm(15, (DEPTH, POOL_WIDTH), 0.1),
        'gla_w_gate': nrm(16, (DEPTH, GLA_RANK, GLA_DK), GLA_RANK ** -0.5),
        'gla_b_gate': nrm(17, (DEPTH, GLA_DK), 0.1),
        'gla_norm': 1.0 + nrm(18, (DEPTH, GLA_HV), 0.05),
        'w_br_pool': nrm(19, (DEPTH, POOL_WIDTH, D_MODEL), BETA * POOL_WIDTH ** -0.5),
        'w_br_attn': nrm(20, (DEPTH, ATTN_OUT, D_MODEL), BETA * ATTN_OUT ** -0.5),
        'w_br_gla': nrm(21, (DEPTH, GLA_DV, D_MODEL), BETA * GLA_DV ** -0.5),
        'w_out': nrm(22, (DEPTH, D_MODEL, D_MODEL), BETA * D_MODEL ** -0.5),
    }


def reference(x_prompt, x_sample, cache_pool, cache_kv_w128, cache_kv_w512, cache_kv_w2048, state_gla,
              rel_bias, ln_gain, ln_bias, ffn_w_gate, ffn_w_up, ffn_w_down, w_in, pool_w, pool_scale,
              gla_w_gate, gla_b_gate, gla_norm, w_br_pool, w_br_attn, w_br_gla, w_out):
    def trunk(x, p0, pool_c, kv_c, gla_c):
        pools, kvs, glas = [], [], []
        for l in range(DEPTH):
            lw = {'ln_gain': ln_gain[l], 'ln_bias': ln_bias[l], 'ffn_w_gate': ffn_w_gate[l],
                  'ffn_w_up': ffn_w_up[l], 'ffn_w_down': ffn_w_down[l], 'w_in': w_in[l],
                  'pool_w': pool_w[l], 'pool_scale': pool_scale[l], 'gla_w_gate': gla_w_gate[l],
                  'gla_b_gate': gla_b_gate[l], 'gla_norm': gla_norm[l], 'w_br_pool': w_br_pool[l],
                  'w_br_attn': w_br_attn[l], 'w_br_gla': w_br_gla[l], 'w_out': w_out[l]}
            x, pb, kvb, S = _layer(x, p0, pool_c[l], [c[l] for c in kv_c], gla_c[l], rel_bias, lw)
            pools.append(pb)
            kvs.append(kvb)
            glas.append(S)
        kv_new = [jnp.stack([kv[g] for kv in kvs]) for g in range(N_DIL_GROUPS)]
        return x, jnp.stack(pools), kv_new, jnp.stack(glas)

    bp = x_prompt.shape[0]
    dt = x_prompt.dtype
    zero_pool = jnp.zeros((DEPTH, bp, POOL_KEEP, POOL_WIDTH), dt)
    zero_kv = [jnp.zeros((DEPTH, bp, W, 2, HEADS_PER_GROUP, HEAD_DIM), dt) for W, _ in DILATED_GROUPS]
    zero_gla = jnp.zeros((DEPTH, bp, GLA_HEADS, GLA_HK, GLA_HV), dt)
    y_prompt, pool_p, kv_p, gla_p = trunk(x_prompt, 0, zero_pool, zero_kv, zero_gla)
    y_sample, pool_s, kv_s, gla_s = trunk(x_sample, PAST_LEN, cache_pool,
                                          [cache_kv_w128, cache_kv_w512, cache_kv_w2048], state_gla)
    return (y_prompt, y_sample, pool_p, kv_p[0], kv_p[1], kv_p[2], gla_p,
            pool_s, kv_s[0], kv_s[1], kv_s[2], gla_s)
```

```python
import functools
import math

import numpy as np
import jax
import jax.numpy as jnp
from jax import lax
from jax.experimental import pallas as pl
from jax.experimental.pallas import tpu as pltpu

DEPTH = 2
PAST_LEN = 16384
POOL_WINDOWS = (2, 4, 8, 16)
POOL_KEEP = max(POOL_WINDOWS) - 1
DILATED_GROUPS = ((128, 1), (512, 4), (2048, 16))
N_GROUPS = len(DILATED_GROUPS)
HEADS = 4
HEAD_DIM = 128
NUM_BUCKETS = 32
MAX_DISTANCE = 2048
GLA_RANK = 16
GLA_TAU = 16.0
GLA_CHUNK = 64
ALPHA = (2 * DEPTH) ** 0.25
LN_EPS = 1e-5
RMS_EPS = 1e-6
NEG_INF = -1e30

LANES = 128
SUBLANES = 8
MXU_DIM = 256
VMEM_LIMIT = 60 * 1024 * 1024

BF16 = jnp.bfloat16
F32 = jnp.float32

GLA_HK_PAD = 256
ATTN_BLOCK = 128
POOL_HALO = 16


def _params(sem):
    return pltpu.CompilerParams(dimension_semantics=sem, vmem_limit_bytes=VMEM_LIMIT)


def _dot(a, b):
    return jnp.dot(a, b, preferred_element_type=F32)


def _dot_nt(a, b):
    return lax.dot_general(a, b, (((1,), (1,)), ((), ())), preferred_element_type=F32)


def _dot_tn(a, b):
    return lax.dot_general(a, b, (((0,), (0,)), ((), ())), preferred_element_type=F32)


def _row_tile(m, pref):
    return pref if m % pref == 0 else m


def _gateup_body(x_ref, wg_ref, wu_ref, o_ref):
    x = x_ref[...]
    g = _dot(x, wg_ref[...])
    u = _dot(x, wu_ref[...])
    o_ref[...] = (jax.nn.silu(g) * u).astype(o_ref.dtype)


def _ffn_hidden(xb, wg, wu, tm_pref=1024, tn=256):
    m, d = xb.shape
    f = wg.shape[1]
    tm = _row_tile(m, tm_pref)
    return pl.pallas_call(
        _gateup_body,
        grid=(m // tm, f // tn),
        in_specs=[pl.BlockSpec((tm, d), lambda i, j: (i, 0)),
                  pl.BlockSpec((d, tn), lambda i, j: (0, j)),
                  pl.BlockSpec((d, tn), lambda i, j: (0, j))],
        out_specs=pl.BlockSpec((tm, tn), lambda i, j: (i, j)),
        out_shape=jax.ShapeDtypeStruct((m, f), BF16),
        compiler_params=_params(("parallel", "arbitrary")),
        name="ffn_hidden",
    )(xb, wg, wu)


def _plain_body(x_ref, w_ref, o_ref):
    o_ref[...] = _dot(x_ref[...], w_ref[...]).astype(o_ref.dtype)


def _matmul(xb, w, tm_pref=1024, tn=512, out_dtype=F32):
    m, d = xb.shape
    n = w.shape[1]
    tm = _row_tile(m, tm_pref)
    return pl.pallas_call(
        _plain_body,
        grid=(m // tm, n // tn),
        in_specs=[pl.BlockSpec((tm, d), lambda i, j: (i, 0)),
                  pl.BlockSpec((d, tn), lambda i, j: (0, j))],
        out_specs=pl.BlockSpec((tm, tn), lambda i, j: (i, j)),
        out_shape=jax.ShapeDtypeStruct((m, n), out_dtype),
        compiler_params=_params(("parallel", "arbitrary")),
        name="matmul",
    )(xb, w)


LN_ROWS = 16


def _res_ln_body(a_ref, w_ref, x_ref, g_ref, b_ref, of_ref, ob_ref, *, coef, nk):
    k = pl.program_id(1)
    part = _dot(a_ref[...], w_ref[...])

    @pl.when(k == 0)
    def _():
        of_ref[...] = part

    @pl.when(k > 0)
    def _():
        of_ref[...] += part

    @pl.when(k == nk - 1)
    def _():
        gain = g_ref[...]
        bias = b_ref[...]

        def rows(r, carry):
            sl = pl.ds(pl.multiple_of(r * LN_ROWS, LN_ROWS), LN_ROWS)
            v = ALPHA * x_ref[sl, :] + coef * of_ref[sl, :]
            mu = jnp.mean(v, axis=-1, keepdims=True)
            c = v - mu
            var = jnp.mean(c * c, axis=-1, keepdims=True)
            y = c * lax.rsqrt(var + LN_EPS) * gain + bias
            of_ref[sl, :] = y
            ob_ref[sl, :] = y.astype(ob_ref.dtype)
            return carry

        lax.fori_loop(0, of_ref.shape[0] // LN_ROWS, rows, 0)


def _matmul_res_ln(a, w, x, gain, bias, coef, tm_pref=512, tk=256):
    m, kdim = a.shape
    d = w.shape[1]
    tm = _row_tile(m, tm_pref)
    nk = kdim // tk
    return pl.pallas_call(
        functools.partial(_res_ln_body, coef=coef, nk=nk),
        grid=(m // tm, nk),
        in_specs=[pl.BlockSpec((tm, tk), lambda i, k: (i, k)),
                  pl.BlockSpec((tk, d), lambda i, k: (k, 0)),
                  pl.BlockSpec((tm, d), lambda i, k: (i, 0)),
                  pl.BlockSpec((1, d), lambda i, k: (0, 0)),
                  pl.BlockSpec((1, d), lambda i, k: (0, 0))],
        out_specs=[pl.BlockSpec((tm, d), lambda i, k: (i, 0)),
                   pl.BlockSpec((tm, d), lambda i, k: (i, 0))],
        out_shape=[jax.ShapeDtypeStruct((m, d), F32), jax.ShapeDtypeStruct((m, d), BF16)],
        compiler_params=_params(("parallel", "arbitrary")),
        name="matmul_res_ln",
    )(a, w, x, gain.reshape(1, d), bias.reshape(1, d))


def _merge_body(yp_ref, ya_ref, yg_ref, wp_ref, wa_ref, wg_ref, gp_ref, ga_ref, gg_ref, o_ref):
    acc = jax.nn.sigmoid(gp_ref[...]) * _dot(yp_ref[...], wp_ref[...])
    acc += jax.nn.sigmoid(ga_ref[...]) * _dot(ya_ref[...], wa_ref[...])
    acc += jax.nn.sigmoid(gg_ref[...]) * _dot(yg_ref[...], wg_ref[...])
    o_ref[...] = acc.astype(o_ref.dtype)


def _merge(y_pool, y_attn, y_gla, w_pool, w_attn, w_gla, gates, tm_pref=1024, tn=512):
    m = y_pool.shape[0]
    d = w_pool.shape[1]
    tm = _row_tile(m, tm_pref)
    nd = d // tn

    def yspec(y):
        return pl.BlockSpec((tm, y.shape[1]), lambda i, j: (i, 0))

    def wspec(w):
        return pl.BlockSpec((w.shape[0], tn), lambda i, j: (0, j))

    def gspec(n):
        return pl.BlockSpec((tm, tn), lambda i, j: (i, n * nd + j))

    return pl.pallas_call(
        _merge_body,
        grid=(m // tm, nd),
        in_specs=[yspec(y_pool), yspec(y_attn), yspec(y_gla), wspec(w_pool), wspec(w_attn), wspec(w_gla),
                  gspec(0), gspec(1), gspec(2)],
        out_specs=pl.BlockSpec((tm, tn), lambda i, j: (i, j)),
        out_shape=jax.ShapeDtypeStruct((m, d), BF16),
        compiler_params=_params(("parallel", "arbitrary")),
        name="merge",
    )(y_pool, y_attn, y_gla, w_pool, w_attn, w_gla, gates, gates, gates)


def _pool_body(u_ref, w_ref, s_ref, o_ref, *, p0, rows):
    t = o_ref.shape[1]
    gi = pl.program_id(1)
    for idx, win in enumerate(POOL_WINDOWS):
        @pl.when(gi == idx)
        def _(win=win):
            wmat = w_ref[0]
            scale = s_ref[0]
            for r0 in range(0, t, rows):
                cur = u_ref[0, pl.ds(POOL_HALO + r0, rows), :]
                tot = cur
                for s in range(1, win):
                    tot = tot + u_ref[0, pl.ds(POOL_HALO + r0 - s, rows), :]
                pos = p0 + r0 + lax.broadcasted_iota(jnp.int32, (rows, 1), 0)
                cnt = jnp.minimum(pos + 1, win).astype(F32)
                dlt = tot / cnt - cur
                y = _dot(dlt.astype(BF16), wmat) * scale
                o_ref[0, pl.ds(r0, rows), :] = y.astype(o_ref.dtype)


def _pool(u_ext, w_grp, scale, p0):
    b, lt, c = u_ext.shape
    t = lt - POOL_HALO
    g = len(POOL_WINDOWS)
    gc = c // g
    rows = min(t, 256)
    return pl.pallas_call(
        functools.partial(_pool_body, p0=p0, rows=rows),
        grid=(b, g),
        in_specs=[pl.BlockSpec((1, lt, gc), lambda i, j: (i, 0, j)),
                  pl.BlockSpec((1, gc, gc), lambda i, j: (j, 0, 0)),
                  pl.BlockSpec((1, 1, gc), lambda i, j: (j, 0, 0))],
        out_specs=pl.BlockSpec((1, t, gc), lambda i, j: (i, 0, j)),
        out_shape=jax.ShapeDtypeStruct((b, t, c), BF16),
        compiler_params=_params(("parallel", "arbitrary")),
        name="pool",
    )(u_ext, w_grp, scale.reshape(g, 1, gc))


def _t5_bias(rel_bias, g):
    w, d = DILATED_GROUPS[g]
    dist = d * np.arange(w // d + 1)
    max_exact = NUM_BUCKETS // 2
    df = jnp.maximum(dist, 1).astype(F32)
    large = max_exact + (jnp.log(df / max_exact) / math.log(MAX_DISTANCE / max_exact)
                         * (NUM_BUCKETS - max_exact)).astype(jnp.int32)
    bucket = jnp.where(dist < max_exact, dist, jnp.minimum(large, NUM_BUCKETS - 1))
    return rel_bias[bucket][:, g * HEADS:(g + 1) * HEADS].T.astype(F32)


def _band_tables(rel_bias):
    a = np.arange(ATTN_BLOCK)[:, None]
    b = np.arange(ATTN_BLOCK)[None, :]
    k_cur = a - b
    k_prev = a + ATTN_BLOCK - b
    tabs = []
    for g in range(N_GROUPS):
        bias = _t5_bias(rel_bias, g)
        cur = jnp.where(k_cur >= 0, bias[:, np.clip(k_cur, 0, ATTN_BLOCK)], NEG_INF)
        prev = jnp.where(k_prev <= ATTN_BLOCK, bias[:, np.clip(k_prev, 0, ATTN_BLOCK)], NEG_INF)
        tabs.append(jnp.stack([cur, prev], axis=1))
    return jnp.stack(tabs)


def _softmax_block(scores, values):
    m = scores[0].max(axis=-1, keepdims=True)
    for s in scores[1:]:
        m = jnp.maximum(m, s.max(axis=-1, keepdims=True))
    l = None
    o = None
    for s, v in zip(scores, values):
        p = jnp.exp(s - m)
        ls = p.sum(axis=-1, keepdims=True)
        os_ = _dot(p.astype(BF16), v.astype(BF16))
        l = ls if l is None else l + ls
        o = os_ if o is None else o + os_
    return m, l, o


def _attn_prompt_body(q0, k0, v0, q1, k1, v1, q2, k2, v2, tab_ref, o_ref, oacc, macc, lacc, *, t):
    qkv = ((q0, k0, v0), (q1, k1, v1), (q2, k2, v2))
    scale = HEAD_DIM ** -0.5
    for g, (_, d) in enumerate(DILATED_GROUPS):
        q_ref, k_ref, v_ref = qkv[g]
        span = ATTN_BLOCK * d

        def rows(start, ref):
            if d == 1:
                return ref[0, pl.ds(start, ATTN_BLOCK), :]
            return ref[0, pl.ds(start, ATTN_BLOCK, stride=d), :]

        for s in range(t // span):
            for r in range(d):
                start = s * span + r
                qb = rows(start, q_ref).astype(BF16)
                scores = [_dot_nt(qb, rows(start, k_ref).astype(BF16)) * scale + tab_ref[g, 0, 0]]
                values = [rows(start, v_ref)]
                if s > 0:
                    scores.append(_dot_nt(qb, rows(start - span, k_ref).astype(BF16)) * scale + tab_ref[g, 0, 1])
                    values.append(rows(start - span, v_ref))
                m, l, o = _softmax_block(scores, values)
                if d == 1:
                    sl = pl.ds(start, ATTN_BLOCK)
                else:
                    sl = pl.ds(start, ATTN_BLOCK, stride=d)
                oacc[g, sl, :] = o
                macc[g, sl, :] = jnp.broadcast_to(m, (ATTN_BLOCK, LANES))
                lacc[g, sl, :] = jnp.broadcast_to(l, (ATTN_BLOCK, LANES))
    for r0 in range(0, t, ATTN_BLOCK):
        sl = pl.ds(r0, ATTN_BLOCK)
        ms = [macc[g, sl, :] for g in range(N_GROUPS)]
        mm = jnp.maximum(jnp.maximum(ms[0], ms[1]), ms[2])
        den = None
        num = None
        for g in range(N_GROUPS):
            c = jnp.exp(ms[g] - mm)
            dn = lacc[g, sl, :] * c
            nm = oacc[g, sl, :] * c
            den = dn if den is None else den + dn
            num = nm if num is None else num + nm
        o_ref[0, sl, :] = (num / den).astype(o_ref.dtype)


def _attn_prompt(h3, tabs, col_q, col_k, col_v):
    b, t, _ = h3.shape
    assert t % (ATTN_BLOCK * max(d for _, d in DILATED_GROUPS)) == 0

    def spec(col, g):
        blk = col // HEAD_DIM + HEADS * g
        return pl.BlockSpec((1, t, HEAD_DIM), lambda i, h, blk=blk: (i, 0, blk + h))

    in_specs = []
    for g in range(N_GROUPS):
        in_specs += [spec(col_q, g), spec(col_k, g), spec(col_v, g)]
    in_specs.append(pl.BlockSpec((N_GROUPS, 1, 2, ATTN_BLOCK, ATTN_BLOCK), lambda i, h: (0, h, 0, 0, 0)))
    return pl.pallas_call(
        functools.partial(_attn_prompt_body, t=t),
        grid=(b, HEADS),
        in_specs=in_specs,
        out_specs=pl.BlockSpec((1, t, HEAD_DIM), lambda i, h: (i, 0, h)),
        out_shape=jax.ShapeDtypeStruct((b, t, HEADS * HEAD_DIM), BF16),
        scratch_shapes=[pltpu.VMEM((N_GROUPS, t, HEAD_DIM), F32),
                        pltpu.VMEM((N_GROUPS, t, LANES), F32),
                        pltpu.VMEM((N_GROUPS, t, LANES), F32)],
        compiler_params=_params(("parallel", "arbitrary")),
        name="attn_prompt",
    )(*([h3] * 9), tabs)


def _decode_tables(rel_bias, t_new, t_pad):
    tabs_c, tabs_n = [], []
    for g, (w, d) in enumerate(DILATED_GROUPS):
        bias = _t5_bias(rel_bias, g)
        tq = np.arange(t_pad)[:, None]
        for keys, out in ((np.arange(w)[None, :], tabs_c), (w + np.arange(t_pad)[None, :], tabs_n)):
            dist = w + tq - keys
            ok = (dist >= 0) & (dist % d == 0) & (dist <= w) & (tq < t_new) & (keys < w + t_new)
            kidx = np.clip(dist // d, 0, w // d)
            out.append(jnp.where(ok, bias[:, kidx], NEG_INF))
    return tabs_c, tabs_n


def _attn_decode_body(q_ref, kn_ref, vn_ref, c0, c1, c2, tc0, tc1, tc2, tn0, tn1, tn2, o_ref):
    caches = (c0, c1, c2)
    tcs = (tc0, tc1, tc2)
    tns = (tn0, tn1, tn2)
    scale = HEAD_DIM ** -0.5
    hw = HEADS * HEAD_DIM
    for h in range(HEADS):
        stats = []
        for g in range(N_GROUPS):
            col = pl.ds((g * HEADS + h) * HEAD_DIM, HEAD_DIM)
            qb = q_ref[0, :, col].astype(BF16)
            kc = caches[g][0, :, pl.ds(h * HEAD_DIM, HEAD_DIM)]
            vc = caches[g][0, :, pl.ds(hw + h * HEAD_DIM, HEAD_DIM)]
            scores = [_dot_nt(qb, kc.astype(BF16)) * scale + tcs[g][h],
                      _dot_nt(qb, kn_ref[0, :, col].astype(BF16)) * scale + tns[g][h]]
            stats.append(_softmax_block(scores, [vc, vn_ref[0, :, col]]))
        mm = jnp.maximum(jnp.maximum(stats[0][0], stats[1][0]), stats[2][0])
        den = None
        num = None
        for m, l, o in stats:
            c = jnp.exp(m - mm)
            den = l * c if den is None else den + l * c
            num = o * c if num is None else num + o * c
        o_ref[0, :, pl.ds(h * HEAD_DIM, HEAD_DIM)] = (num / den).astype(o_ref.dtype)


def _attn_decode(q, kn, vn, caches, tabs_c, tabs_n):
    b, tp, _ = q.shape

    def full(a):
        nd = a.ndim
        return pl.BlockSpec(a.shape, lambda i, nd=nd: (0,) * nd)

    def per_batch(a):
        return pl.BlockSpec((1,) + a.shape[1:], lambda i: (i, 0, 0))

    args = [q, kn, vn, *caches, *tabs_c, *tabs_n]
    in_specs = [per_batch(a) for a in args[:6]] + [full(a) for a in args[6:]]
    return pl.pallas_call(
        _attn_decode_body,
        grid=(b,),
        in_specs=in_specs,
        out_specs=pl.BlockSpec((1, tp, HEADS * HEAD_DIM), lambda i: (i, 0, 0)),
        out_shape=jax.ShapeDtypeStruct((b, tp, HEADS * HEAD_DIM), BF16),
        compiler_params=_params(("parallel",)),
        name="attn_decode",
    )(*args)


def _gla_body(q_ref, k_ref, v_ref, r_ref, low_ref, wg_ref, bg_ref, nrm_ref, s0_ref, y_ref, s_ref, st,
              *, chunk, valid, hk):
    t = q_ref.shape[1]
    st[...] = s0_ref[0, 0]
    wgate = wg_ref[0]
    bgate = bg_ref[0]
    nrm = nrm_ref[...]
    row = lax.broadcasted_iota(jnp.int32, (chunk, chunk), 0)
    colm = lax.broadcasted_iota(jnp.int32, (chunk, chunk), 1)
    causal = row >= colm
    tri = causal.astype(F32)
    live = lax.broadcasted_iota(jnp.int32, (chunk, 1), 0) < valid

    def step(c, carry):
        sl = pl.ds(pl.multiple_of(c * chunk, chunk), chunk)
        z = _dot(low_ref[0, sl, :].astype(BF16), wgate) + bgate
        glog = jnp.where(live, jax.nn.log_sigmoid(z) / GLA_TAU, 0.0)
        gcum = jnp.dot(tri, glog, precision=lax.Precision.HIGHEST, preferred_element_type=F32)
        glast = gcum[chunk - 1:chunk, :]
        kk = k_ref[0, sl, :]
        vv = v_ref[0, sl, :].astype(BF16)
        qg = (q_ref[0, sl, :] * (hk ** -0.5)) * jnp.exp(gcum)
        qgb = qg.astype(BF16)
        a = _dot_nt(qgb, (kk * jnp.exp(-gcum)).astype(BF16))
        a = jnp.where(causal, a, 0.0)
        s_old = st[...]
        o = _dot_nt(qgb, s_old.astype(BF16)) + _dot(a.astype(BF16), vv)
        st[...] = jnp.exp(glast) * s_old + _dot_tn(vv, (kk * jnp.exp(glast - gcum)).astype(BF16))
        o = o * lax.rsqrt(jnp.mean(o * o, axis=-1, keepdims=True) + RMS_EPS) * nrm
        y_ref[0, sl, :] = (o * jax.nn.silu(r_ref[0, sl, :])).astype(y_ref.dtype)
        return carry

    lax.fori_loop(0, t // chunk, step, 0)
    s_ref[0, 0] = st[...]


def _gla(h3, wgate, bgate, nrm, s0t, cols, chunk, valid, hk):
    b, t, _ = h3.shape
    hv = nrm.shape[-1]
    col_q, col_k, col_v, col_r, col_low = cols

    def spec(col, width):
        blk = col // width
        return pl.BlockSpec((1, t, width), lambda i, h, blk=blk: (i, 0, blk + h))

    return pl.pallas_call(
        functools.partial(_gla_body, chunk=chunk, valid=valid, hk=hk),
        grid=(b, HEADS),
        in_specs=[spec(col_q, GLA_HK_PAD), spec(col_k, GLA_HK_PAD), spec(col_v, hv), spec(col_r, hv),
                  pl.BlockSpec((1, t, LANES), lambda i, h: (i, 0, col_low // LANES)),
                  pl.BlockSpec((1, LANES, GLA_HK_PAD), lambda i, h: (h, 0, 0)),
                  pl.BlockSpec((1, 1, GLA_HK_PAD), lambda i, h: (h, 0, 0)),
                  pl.BlockSpec((1, hv), lambda i, h: (0, 0)),
                  pl.BlockSpec((1, 1, hv, GLA_HK_PAD), lambda i, h: (i, h, 0, 0))],
        out_specs=[pl.BlockSpec((1, t, hv), lambda i, h: (i, 0, h)),
                   pl.BlockSpec((1, 1, hv, GLA_HK_PAD), lambda i, h: (i, h, 0, 0))],
        out_shape=[jax.ShapeDtypeStruct((b, t, HEADS * hv), BF16),
                   jax.ShapeDtypeStruct((b, HEADS, hv, GLA_HK_PAD), F32)],
        scratch_shapes=[pltpu.VMEM((hv, GLA_HK_PAD), F32)],
        compiler_params=_params(("parallel", "arbitrary")),
        name="gla",
    )(h3, h3, h3, h3, h3, wgate, bgate, nrm.reshape(1, hv), s0t)


def _pad_cols(a, width):
    return jnp.pad(a, ((0, 0), (0, width - a.shape[1])))


def _pad_heads(a, hk):
    r = a.shape[0]
    return jnp.pad(a.reshape(r, HEADS, hk), ((0, 0), (0, 0), (0, GLA_HK_PAD - hk))).reshape(r, HEADS * GLA_HK_PAD)


def _layout(d_model):
    pool_w = 3 * d_model // 8
    attn_w = N_GROUPS * HEADS * HEAD_DIM
    dv = 3 * d_model // 8
    dk = dv // 2
    sections = (pool_w, attn_w, attn_w, attn_w, dk, dk, dv, GLA_RANK, dv, 3 * d_model)
    src = {}
    acc = 0
    for name, w in zip(("u", "aq", "ak", "av", "gq", "gk", "gv", "glow", "gr", "gates"), sections):
        src[name] = (acc, w)
        acc += w
    order = ("u", "aq", "ak", "av", "gv", "gr", "gq", "gk", "glow")
    widths = {"u": pool_w, "aq": attn_w, "ak": attn_w, "av": attn_w, "gv": dv, "gr": dv,
              "gq": HEADS * GLA_HK_PAD, "gk": HEADS * GLA_HK_PAD, "glow": 512}
    dst = {}
    acc = 0
    for name in order:
        dst[name] = acc
        acc += widths[name]
    return src, dst, widths, order, dk // HEADS, dv // HEADS


def _prep_layer(l, d_model, ffn_w_gate, ffn_w_up, ffn_w_down, w_in, pool_w, gla_w_gate, gla_b_gate,
                w_br_pool, w_br_attn, w_br_gla, w_out):
    src, dst, widths, order, hk, hv = _layout(d_model)
    wl = w_in[l]
    parts = []
    for name in order:
        off, w = src[name]
        blk = wl[:, off:off + w]
        if name in ("gq", "gk"):
            blk = _pad_heads(blk, hk)
        elif name == "glow":
            blk = _pad_cols(blk, widths[name])
        parts.append(blk)
    off, w = src["gates"]
    wgate = jnp.pad(gla_w_gate[l], ((0, LANES - GLA_RANK), (0, 0)))
    wgate = _pad_heads(wgate, hk).reshape(LANES, HEADS, GLA_HK_PAD).transpose(1, 0, 2)
    bgate = _pad_heads(gla_b_gate[l].reshape(1, -1), hk).reshape(HEADS, 1, GLA_HK_PAD)
    return dict(
        wg=[ffn_w_gate[l, i].astype(BF16) for i in range(2)],
        wu=[ffn_w_up[l, i].astype(BF16) for i in range(2)],
        wd=[ffn_w_down[l, i].astype(BF16) for i in range(2)],
        w_a=jnp.concatenate(parts, axis=1).astype(BF16),
        w_gates=wl[:, off:off + w].astype(BF16),
        pool_w=pool_w[l].astype(BF16),
        gla_wgate=wgate.astype(BF16),
        gla_bgate=bgate.astype(F32),
        w_br_pool=w_br_pool[l].astype(BF16),
        w_br_attn=w_br_attn[l].astype(BF16),
        w_br_gla=w_br_gla[l].astype(BF16),
        w_out=w_out[l].astype(BF16),
    )


def _ffn_block(x, xb, pw, i, gain, bias):
    hid = _ffn_hidden(xb, pw["wg"][i], pw["wu"][i])
    return _matmul_res_ln(hid, pw["wd"][i], x, gain, bias, 0.5, tk=256)


def _kv_rows(h3, dst, g, rows):
    gw = HEADS * HEAD_DIM
    b = h3.shape[0]
    k = h3[:, rows, dst["ak"] + g * gw: dst["ak"] + (g + 1) * gw]
    v = h3[:, rows, dst["av"] + g * gw: dst["av"] + (g + 1) * gw]
    return jnp.stack([k, v], axis=2).reshape(b, k.shape[1], 2, HEADS, HEAD_DIM)


def _layer(x, bsz, t_new, decode, pw, lw, caches):
    m, d_model = x.shape
    src, dst, widths, order, hk, hv = _layout(d_model)
    xb = x.astype(BF16)
    x, xb = _ffn_block(x, xb, pw, 0, lw["ln_gain"][0], lw["ln_bias"][0])
    h_a = _matmul(xb, pw["w_a"])
    gates = _matmul(xb, pw["w_gates"])
    h3 = h_a.reshape(bsz, t_new, -1)
    pool_c = widths["u"]
    u = h3[:, :, dst["u"]:dst["u"] + pool_c]
    if decode:
        pool_buf, kv_bufs, gla_s = caches
        t_pad = SUBLANES
        p0 = PAST_LEN
        h3p = jnp.pad(h3, ((0, 0), (0, t_pad - t_new), (0, 0)))
        hist = pool_buf
    else:
        t_pad = t_new
        p0 = 0
        h3p = h3
        hist = jnp.zeros((bsz, POOL_KEEP, pool_c), F32)
        gla_s = jnp.zeros((bsz, HEADS, hk, hv), F32)
    u_ext = jnp.concatenate([jnp.zeros((bsz, POOL_HALO - POOL_KEEP, pool_c), F32), hist, u,
                             jnp.zeros((bsz, t_pad - t_new, pool_c), F32)], axis=1)
    y_pool = _pool(u_ext, pw["pool_w"], lw["pool_scale"], p0)[:, :t_new]
    new_pool = jnp.concatenate([hist, u], axis=1)[:, -POOL_KEEP:]
    aw = widths["aq"]
    if decode:
        q, kn, vn = (h3p[:, :, dst[n]:dst[n] + aw] for n in ("aq", "ak", "av"))
        flat = [c.reshape(bsz, c.shape[1], 2 * HEADS * HEAD_DIM) for c in kv_bufs]
        y_attn = _attn_decode(q, kn, vn, flat, *lw["decode_tabs"])[:, :t_new]
        new_kv = [jnp.concatenate([c[:, t_new:], _kv_rows(h3, dst, g, slice(None))], axis=1)
                  for g, c in enumerate(kv_bufs)]
    else:
        y_attn = _attn_prompt(h3, lw["band_tabs"], dst["aq"], dst["ak"], dst["av"])
        new_kv = [_kv_rows(h3, dst, g, slice(t_new - min(w, t_new), None))
                  for g, (w, _) in enumerate(DILATED_GROUPS)]
    s0t = jnp.pad(jnp.swapaxes(gla_s.astype(F32), -1, -2), ((0, 0), (0, 0), (0, 0), (0, GLA_HK_PAD - hk)))
    chunk = SUBLANES if decode else math.gcd(t_new, GLA_CHUNK)
    y_gla, st = _gla(h3p, pw["gla_wgate"], pw["gla_bgate"], lw["gla_norm"], s0t,
                     (dst["gq"], dst["gk"], dst["gv"], dst["gr"], dst["glow"]), chunk,
                     t_new if decode else chunk, hk)
    y_gla = y_gla[:, :t_new]
    new_s = jnp.swapaxes(st[..., :hk], -1, -2)
    merged = _merge(y_pool.reshape(m, -1), y_attn.reshape(m, -1), y_gla.reshape(m, -1),
                    pw["w_br_pool"], pw["w_br_attn"], pw["w_br_gla"], gates)
    x, xb = _matmul_res_ln(merged, pw["w_out"], x, lw["ln_gain"][1], lw["ln_bias"][1], 1.0, tk=256)
    x, xb = _ffn_block(x, xb, pw, 1, lw["ln_gain"][2], lw["ln_bias"][2])
    return x, new_pool, new_kv, new_s


def _trunk(x3, decode, prepped, small, caches):
    bsz, t_new, d_model = x3.shape
    x = x3.reshape(bsz * t_new, d_model)
    pools, kvs, glas = [], [], []
    for l in range(DEPTH):
        lc = None
        if decode:
            lc = (caches[0][l], [c[l] for c in caches[1]], caches[2][l])
        x, pb, kvb, s = _layer(x, bsz, t_new, decode, prepped[l], small[l], lc)
        pools.append(pb)
        kvs.append(kvb)
        glas.append(s)
    kv_new = [jnp.stack([kv[g] for kv in kvs]) for g in range(N_GROUPS)]
    return x.reshape(bsz, t_new, d_model), jnp.stack(pools), kv_new, jnp.stack(glas)


def kernel(x_prompt, x_sample, cache_pool, cache_kv_w128, cache_kv_w512, cache_kv_w2048, state_gla, rel_bias, ln_gain, ln_bias, ffn_w_gate, ffn_w_up, ffn_w_down, w_in, pool_w, pool_scale, gla_w_gate, gla_b_gate, gla_norm, w_br_pool, w_br_attn, w_br_gla, w_out):
    d_model = x_prompt.shape[-1]
    prepped = [_prep_layer(l, d_model, ffn_w_gate, ffn_w_up, ffn_w_down, w_in, pool_w, gla_w_gate,
                           gla_b_gate, w_br_pool, w_br_attn, w_br_gla, w_out) for l in range(DEPTH)]
    band_tabs = _band_tables(rel_bias)
    decode_tabs = _decode_tables(rel_bias, x_sample.shape[1], SUBLANES)
    small = [dict(ln_gain=ln_gain[l], ln_bias=ln_bias[l], pool_scale=pool_scale[l], gla_norm=gla_norm[l],
                  band_tabs=band_tabs, decode_tabs=decode_tabs) for l in range(DEPTH)]
    y_p, pool_p, kv_p, gla_p = _trunk(x_prompt, False, prepped, small, None)
    y_s, pool_s, kv_s, gla_s = _trunk(x_sample, True, prepped, small,
                                      (cache_pool, [cache_kv_w128, cache_kv_w512, cache_kv_w2048], state_gla))
    return (y_p, y_s, pool_p, kv_p[0], kv_p[1], kv_p[2], gla_p,
            pool_s, kv_s[0], kv_s[1], kv_s[2], gla_s)
```

```python
import functools
import math

import numpy as np
import jax
import jax.numpy as jnp
from jax import lax
from jax.experimental import pallas as pl
from jax.experimental.pallas import tpu as pltpu

DEPTH = 2
PAST_LEN = 16384
POOL_WINDOWS = (2, 4, 8, 16)
POOL_KEEP = max(POOL_WINDOWS) - 1
DILATED_GROUPS = ((128, 1), (512, 4), (2048, 16))
N_GROUPS = len(DILATED_GROUPS)
HEADS = 4
HEAD_DIM = 128
NUM_BUCKETS = 32
MAX_DISTANCE = 2048
GLA_RANK = 16
GLA_TAU = 16.0
GLA_CHUNK = 64
ALPHA = (2 * DEPTH) ** 0.25
LN_EPS = 1e-5
RMS_EPS = 1e-6
NEG_INF = -1e30

LANES = 128
SUBLANES = 8
MXU_DIM = 256
VMEM_LIMIT = 60 * 1024 * 1024

BF16 = jnp.bfloat16
F32 = jnp.float32

GLA_HK_PAD = 256
ATTN_BLOCK = 128
POOL_HALO = 16


def _params(sem):
    return pltpu.CompilerParams(dimension_semantics=sem, vmem_limit_bytes=VMEM_LIMIT)


def _dot(a, b):
    return jnp.dot(a, b, preferred_element_type=F32)


def _dot_nt(a, b):
    return lax.dot_general(a, b, (((1,), (1,)), ((), ())), preferred_element_type=F32)


def _dot_tn(a, b):
    return lax.dot_general(a, b, (((0,), (0,)), ((), ())), preferred_element_type=F32)


def _row_tile(m, pref):
    return pref if m % pref == 0 else m


def _wspec(prefix, rows, cols, index):
    return pl.BlockSpec((None,) * len(prefix) + (rows, cols), lambda *g: tuple(prefix) + tuple(index(*g)))


def _gateup_body(x_ref, wg_ref, wu_ref, o_ref):
    x = x_ref[...]
    g = _dot(x, wg_ref[...].astype(BF16))
    u = _dot(x, wu_ref[...].astype(BF16))
    o_ref[...] = (jax.nn.silu(g) * u).astype(o_ref.dtype)


def _ffn_hidden(xb, wg, wu, prefix, tm_pref=1024, tn=256):
    m, d = xb.shape
    f = wg.shape[-1]
    tm = _row_tile(m, tm_pref)
    wspec = _wspec(prefix, d, tn, lambda i, j: (0, j))
    return pl.pallas_call(
        _gateup_body,
        grid=(m // tm, f // tn),
        in_specs=[pl.BlockSpec((tm, d), lambda i, j: (i, 0)), wspec, wspec],
        out_specs=pl.BlockSpec((tm, tn), lambda i, j: (i, j)),
        out_shape=jax.ShapeDtypeStruct((m, f), BF16),
        compiler_params=_params(("parallel", "arbitrary")),
        name="ffn_hidden",
    )(xb, wg, wu)


def _plain_body(x_ref, w_ref, o_ref):
    o_ref[...] = _dot(x_ref[...], w_ref[...]).astype(o_ref.dtype)


def _matmul(xb, w, prefix, tm_pref=1024, tn=512, out_dtype=F32):
    m, d = xb.shape
    n = w.shape[-1]
    tm = _row_tile(m, tm_pref)
    return pl.pallas_call(
        _plain_body,
        grid=(m // tm, n // tn),
        in_specs=[pl.BlockSpec((tm, d), lambda i, j: (i, 0)),
                  _wspec(prefix, d, tn, lambda i, j: (0, j))],
        out_specs=pl.BlockSpec((tm, tn), lambda i, j: (i, j)),
        out_shape=jax.ShapeDtypeStruct((m, n), out_dtype),
        compiler_params=_params(("parallel", "arbitrary")),
        name="matmul",
    )(xb, w)


LN_ROWS = 16


def _res_ln_body(a_ref, w_ref, x_ref, g_ref, b_ref, of_ref, ob_ref, *, coef, nkh, nj, tn):
    kh = pl.program_id(1)
    j = pl.program_id(2)
    col = pl.ds(pl.multiple_of(j * tn, tn), tn)
    part = _dot(a_ref[...], w_ref[...])
    if nkh == 1:
        of_ref[:, col] = ALPHA * x_ref[...] + coef * part
    else:
        @pl.when(kh == 0)
        def _():
            of_ref[:, col] = part

        @pl.when((kh > 0) & (kh < nkh - 1))
        def _():
            of_ref[:, col] += part

        @pl.when(kh == nkh - 1)
        def _():
            of_ref[:, col] = ALPHA * x_ref[...] + coef * (of_ref[:, col] + part)

    @pl.when((kh == nkh - 1) & (j == nj - 1))
    def _():
        gain = g_ref[...]
        bias = b_ref[...]

        def rows(r, carry):
            sl = pl.ds(pl.multiple_of(r * LN_ROWS, LN_ROWS), LN_ROWS)
            v = of_ref[sl, :]
            mu = jnp.mean(v, axis=-1, keepdims=True)
            c = v - mu
            var = jnp.mean(c * c, axis=-1, keepdims=True)
            y = c * lax.rsqrt(var + LN_EPS) * gain + bias
            of_ref[sl, :] = y
            ob_ref[sl, :] = y.astype(ob_ref.dtype)
            return carry

        lax.fori_loop(0, of_ref.shape[0] // LN_ROWS, rows, 0)


def _matmul_res_ln(a, w, prefix, x, gain, bias, coef, nkh, tm_pref=512, tn=512):
    m, kdim = a.shape
    d = w.shape[-1]
    tm = _row_tile(m, tm_pref)
    tk = kdim // nkh
    nj = d // tn
    return pl.pallas_call(
        functools.partial(_res_ln_body, coef=coef, nkh=nkh, nj=nj, tn=tn),
        grid=(m // tm, nkh, nj),
        in_specs=[pl.BlockSpec((tm, tk), lambda i, k, j: (i, k)),
                  _wspec(prefix, tk, tn, lambda i, k, j: (k, j)),
                  pl.BlockSpec((tm, tn), lambda i, k, j: (i, jnp.where(k == nkh - 1, j, 0))),
                  pl.BlockSpec((1, d), lambda i, k, j: (0, 0)),
                  pl.BlockSpec((1, d), lambda i, k, j: (0, 0))],
        out_specs=[pl.BlockSpec((tm, d), lambda i, k, j: (i, 0)),
                   pl.BlockSpec((tm, d), lambda i, k, j: (i, 0))],
        out_shape=[jax.ShapeDtypeStruct((m, d), F32), jax.ShapeDtypeStruct((m, d), BF16)],
        compiler_params=_params(("parallel", "arbitrary", "arbitrary")),
        name="matmul_res_ln",
    )(a, w, x, gain.reshape(1, d), bias.reshape(1, d))


def _merge_body(yp_ref, ya_ref, yg_ref, wp_ref, wa_ref, wg_ref, gp_ref, ga_ref, gg_ref, o_ref):
    acc = jax.nn.sigmoid(gp_ref[...]) * _dot(yp_ref[...], wp_ref[...])
    acc += jax.nn.sigmoid(ga_ref[...]) * _dot(ya_ref[...], wa_ref[...])
    acc += jax.nn.sigmoid(gg_ref[...]) * _dot(yg_ref[...], wg_ref[...])
    o_ref[...] = acc.astype(o_ref.dtype)


def _merge(y_pool, y_attn, y_gla, w_pool, w_attn, w_gla, prefix, h, gate_col, tm_pref=1024, tn=512):
    m = y_pool.shape[0]
    d = w_pool.shape[-1]
    tm = _row_tile(m, tm_pref)
    nd = d // tn
    g0 = gate_col // tn

    def yspec(y):
        return pl.BlockSpec((tm, y.shape[1]), lambda i, j: (i, 0))

    def wspec(w):
        return _wspec(prefix, w.shape[-2], tn, lambda i, j: (0, j))

    def gspec(n):
        return pl.BlockSpec((tm, tn), lambda i, j: (i, g0 + n * nd + j))

    return pl.pallas_call(
        _merge_body,
        grid=(m // tm, nd),
        in_specs=[yspec(y_pool), yspec(y_attn), yspec(y_gla), wspec(w_pool), wspec(w_attn), wspec(w_gla),
                  gspec(0), gspec(1), gspec(2)],
        out_specs=pl.BlockSpec((tm, tn), lambda i, j: (i, j)),
        out_shape=jax.ShapeDtypeStruct((m, d), BF16),
        compiler_params=_params(("parallel", "arbitrary")),
        name="merge",
    )(y_pool, y_attn, y_gla, w_pool, w_attn, w_gla, h, h, h)


def _pool_body(u_ref, w_ref, s_ref, o_ref, *, p0, rows):
    t = o_ref.shape[1]
    gi = pl.program_id(1)
    for idx, win in enumerate(POOL_WINDOWS):
        @pl.when(gi == idx)
        def _(win=win):
            wmat = w_ref[0]
            scale = s_ref[0]
            for r0 in range(0, t, rows):
                cur = u_ref[0, pl.ds(POOL_HALO + r0, rows), :]
                tot = cur
                for s in range(1, win):
                    tot = tot + u_ref[0, pl.ds(POOL_HALO + r0 - s, rows), :]
                pos = p0 + r0 + lax.broadcasted_iota(jnp.int32, (rows, 1), 0)
                cnt = jnp.minimum(pos + 1, win).astype(F32)
                dlt = tot / cnt - cur
                y = _dot(dlt.astype(BF16), wmat) * scale
                o_ref[0, pl.ds(r0, rows), :] = y.astype(o_ref.dtype)


def _pool(u_ext, w_grp, scale, p0):
    b, lt, c = u_ext.shape
    t = lt - POOL_HALO
    g = len(POOL_WINDOWS)
    gc = c // g
    rows = min(t, 256)
    return pl.pallas_call(
        functools.partial(_pool_body, p0=p0, rows=rows),
        grid=(b, g),
        in_specs=[pl.BlockSpec((1, lt, gc), lambda i, j: (i, 0, j)),
                  pl.BlockSpec((1, gc, gc), lambda i, j: (j, 0, 0)),
                  pl.BlockSpec((1, 1, gc), lambda i, j: (j, 0, 0))],
        out_specs=pl.BlockSpec((1, t, gc), lambda i, j: (i, 0, j)),
        out_shape=jax.ShapeDtypeStruct((b, t, c), BF16),
        compiler_params=_params(("parallel", "arbitrary")),
        name="pool",
    )(u_ext, w_grp, scale.reshape(g, 1, gc))


def _t5_bias(rel_bias, g):
    w, d = DILATED_GROUPS[g]
    dist = d * np.arange(w // d + 1)
    max_exact = NUM_BUCKETS // 2
    df = jnp.maximum(dist, 1).astype(F32)
    large = max_exact + (jnp.log(df / max_exact) / math.log(MAX_DISTANCE / max_exact)
                         * (NUM_BUCKETS - max_exact)).astype(jnp.int32)
    bucket = jnp.where(dist < max_exact, dist, jnp.minimum(large, NUM_BUCKETS - 1))
    return rel_bias[bucket][:, g * HEADS:(g + 1) * HEADS].T.astype(F32)


def _neg(rows, n):
    return jnp.full((rows, n), NEG_INF, F32)


def _toeplitz(p):
    h, n = p.shape
    b = n // 2
    return jnp.tile(p, (1, b))[:, :b * (n - 1)].reshape(h, b, n - 1)[:, :, :b]


def _band_tables(rel_bias):
    blk = ATTN_BLOCK
    tabs = []
    for g in range(N_GROUPS):
        bias = _t5_bias(rel_bias, g)
        cur = _toeplitz(jnp.concatenate([bias[:, :1], _neg(HEADS, blk), jnp.flip(bias[:, 1:blk], 1)], 1))
        prev = _toeplitz(jnp.concatenate([jnp.flip(bias[:, 1:blk + 1], 1), _neg(HEADS, blk)], 1))
        tabs.append(jnp.stack([cur, prev], axis=1))
    return jnp.stack(tabs)


def _softmax_block(scores, values):
    m = scores[0].max(axis=-1, keepdims=True)
    for s in scores[1:]:
        m = jnp.maximum(m, s.max(axis=-1, keepdims=True))
    l = None
    o = None
    for s, v in zip(scores, values):
        p = jnp.exp(s - m)
        ls = p.sum(axis=-1, keepdims=True)
        os_ = _dot(p.astype(BF16), v.astype(BF16))
        l = ls if l is None else l + ls
        o = os_ if o is None else o + os_
    return m, l, o


def _attn_prompt_body(q0, k0, v0, q1, k1, v1, q2, k2, v2, tab_ref, o_ref, oacc, macc, lacc, *, t):
    qkv = ((q0, k0, v0), (q1, k1, v1), (q2, k2, v2))
    scale = HEAD_DIM ** -0.5
    for g, (_, d) in enumerate(DILATED_GROUPS):
        q_ref, k_ref, v_ref = qkv[g]
        span = ATTN_BLOCK * d

        def rows(start, ref):
            if d == 1:
                return ref[0, pl.ds(start, ATTN_BLOCK), :]
            return ref[0, pl.ds(start, ATTN_BLOCK, stride=d), :]

        for s in range(t // span):
            for r in range(d):
                start = s * span + r
                qb = rows(start, q_ref).astype(BF16)
                scores = [_dot_nt(qb, rows(start, k_ref).astype(BF16)) * scale + tab_ref[g, 0, 0]]
                values = [rows(start, v_ref)]
                if s > 0:
                    scores.append(_dot_nt(qb, rows(start - span, k_ref).astype(BF16)) * scale + tab_ref[g, 0, 1])
                    values.append(rows(start - span, v_ref))
                m, l, o = _softmax_block(scores, values)
                if d == 1:
                    sl = pl.ds(start, ATTN_BLOCK)
                else:
                    sl = pl.ds(start, ATTN_BLOCK, stride=d)
                oacc[g, sl, :] = o
                macc[g, sl, :] = jnp.broadcast_to(m, (ATTN_BLOCK, LANES))
                lacc[g, sl, :] = jnp.broadcast_to(l, (ATTN_BLOCK, LANES))
    for r0 in range(0, t, ATTN_BLOCK):
        sl = pl.ds(r0, ATTN_BLOCK)
        ms = [macc[g, sl, :] for g in range(N_GROUPS)]
        mm = jnp.maximum(jnp.maximum(ms[0], ms[1]), ms[2])
        den = None
        num = None
        for g in range(N_GROUPS):
            c = jnp.exp(ms[g] - mm)
            dn = lacc[g, sl, :] * c
            nm = oacc[g, sl, :] * c
            den = dn if den is None else den + dn
            num = nm if num is None else num + nm
        o_ref[0, sl, :] = (num / den).astype(o_ref.dtype)


def _attn_prompt(h3, tabs, col_q, col_k, col_v):
    b, t, _ = h3.shape
    assert t % (ATTN_BLOCK * max(d for _, d in DILATED_GROUPS)) == 0

    def spec(col, g):
        blk = col // HEAD_DIM + HEADS * g
        return pl.BlockSpec((1, t, HEAD_DIM), lambda i, h, blk=blk: (i, 0, blk + h))

    in_specs = []
    for g in range(N_GROUPS):
        in_specs += [spec(col_q, g), spec(col_k, g), spec(col_v, g)]
    in_specs.append(pl.BlockSpec((N_GROUPS, 1, 2, ATTN_BLOCK, ATTN_BLOCK), lambda i, h: (0, h, 0, 0, 0)))
    return pl.pallas_call(
        functools.partial(_attn_prompt_body, t=t),
        grid=(b, HEADS),
        in_specs=in_specs,
        out_specs=pl.BlockSpec((1, t, HEAD_DIM), lambda i, h: (i, 0, h)),
        out_shape=jax.ShapeDtypeStruct((b, t, HEADS * HEAD_DIM), BF16),
        scratch_shapes=[pltpu.VMEM((N_GROUPS, t, HEAD_DIM), F32),
                        pltpu.VMEM((N_GROUPS, t, LANES), F32),
                        pltpu.VMEM((N_GROUPS, t, LANES), F32)],
        compiler_params=_params(("parallel", "arbitrary")),
        name="attn_prompt",
    )(*([h3] * 9), tabs)


def _decode_tables(rel_bias, t_new, t_pad):
    tabs_c, tabs_n = [], []
    for g, (w, d) in enumerate(DILATED_GROUPS):
        bias = _t5_bias(rel_bias, g)
        by_dist = jnp.pad(bias[:, :, None], ((0, 0), (0, 0), (0, d - 1)), constant_values=NEG_INF)
        by_dist = by_dist.reshape(HEADS, -1)[:, :w + 1]
        rows_c, rows_n = [], []
        for t in range(t_pad):
            if t < t_new:
                rows_c.append(jnp.flip(jnp.concatenate([by_dist[:, t + 1:], _neg(HEADS, t)], 1), 1))
                rows_n.append(jnp.concatenate([jnp.flip(by_dist[:, :t + 1], 1), _neg(HEADS, t_pad - t - 1)], 1))
            else:
                rows_c.append(_neg(HEADS, w))
                rows_n.append(_neg(HEADS, t_pad))
        tabs_c.append(jnp.stack(rows_c, axis=1))
        tabs_n.append(jnp.stack(rows_n, axis=1))
    return tabs_c, tabs_n


def _attn_decode_body(q_ref, kn_ref, vn_ref, c0, c1, c2, tc0, tc1, tc2, tn0, tn1, tn2, o_ref):
    caches = (c0, c1, c2)
    tcs = (tc0, tc1, tc2)
    tns = (tn0, tn1, tn2)
    scale = HEAD_DIM ** -0.5
    hw = HEADS * HEAD_DIM
    for h in range(HEADS):
        stats = []
        for g in range(N_GROUPS):
            col = pl.ds((g * HEADS + h) * HEAD_DIM, HEAD_DIM)
            qb = q_ref[0, :, col].astype(BF16)
            kc = caches[g][0, :, pl.ds(h * HEAD_DIM, HEAD_DIM)]
            vc = caches[g][0, :, pl.ds(hw + h * HEAD_DIM, HEAD_DIM)]
            scores = [_dot_nt(qb, kc.astype(BF16)) * scale + tcs[g][h],
                      _dot_nt(qb, kn_ref[0, :, col].astype(BF16)) * scale + tns[g][h]]
            stats.append(_softmax_block(scores, [vc, vn_ref[0, :, col]]))
        mm = jnp.maximum(jnp.maximum(stats[0][0], stats[1][0]), stats[2][0])
        den = None
        num = None
        for m, l, o in stats:
            c = jnp.exp(m - mm)
            den = l * c if den is None else den + l * c
            num = o * c if num is None else num + o * c
        o_ref[0, :, pl.ds(h * HEAD_DIM, HEAD_DIM)] = (num / den).astype(o_ref.dtype)


def _attn_decode(q, kn, vn, caches, tabs_c, tabs_n):
    b, tp, _ = q.shape

    def full(a):
        nd = a.ndim
        return pl.BlockSpec(a.shape, lambda i, nd=nd: (0,) * nd)

    def per_batch(a):
        return pl.BlockSpec((1,) + a.shape[1:], lambda i: (i, 0, 0))

    args = [q, kn, vn, *caches, *tabs_c, *tabs_n]
    in_specs = [per_batch(a) for a in args[:6]] + [full(a) for a in args[6:]]
    return pl.pallas_call(
        _attn_decode_body,
        grid=(b,),
        in_specs=in_specs,
        out_specs=pl.BlockSpec((1, tp, HEADS * HEAD_DIM), lambda i: (i, 0, 0)),
        out_shape=jax.ShapeDtypeStruct((b, tp, HEADS * HEAD_DIM), BF16),
        compiler_params=_params(("parallel",)),
        name="attn_decode",
    )(*args)


def _gla_body(q_ref, k_ref, v_ref, r_ref, low_ref, wg_ref, bg_ref, nrm_ref, s0_ref, y_ref, s_ref, st,
              *, chunk, valid, hk):
    t = q_ref.shape[1]
    st[...] = s0_ref[0, 0]
    wgate = wg_ref[0]
    bgate = bg_ref[0]
    nrm = nrm_ref[...]
    row = lax.broadcasted_iota(jnp.int32, (chunk, chunk), 0)
    colm = lax.broadcasted_iota(jnp.int32, (chunk, chunk), 1)
    causal = row >= colm
    tri = causal.astype(F32)
    live = lax.broadcasted_iota(jnp.int32, (chunk, 1), 0) < valid

    def step(c, carry):
        sl = pl.ds(pl.multiple_of(c * chunk, chunk), chunk)
        z = _dot(low_ref[0, sl, :].astype(BF16), wgate) + bgate
        glog = jnp.where(live, jax.nn.log_sigmoid(z) / GLA_TAU, 0.0)
        gcum = jnp.dot(tri, glog, precision=lax.Precision.HIGHEST, preferred_element_type=F32)
        glast = gcum[chunk - 1:chunk, :]
        kk = k_ref[0, sl, :]
        vv = v_ref[0, sl, :].astype(BF16)
        qg = (q_ref[0, sl, :] * (hk ** -0.5)) * jnp.exp(gcum)
        qgb = qg.astype(BF16)
        a = _dot_nt(qgb, (kk * jnp.exp(-gcum)).astype(BF16))
        a = jnp.where(causal, a, 0.0)
        s_old = st[...]
        o = _dot_nt(qgb, s_old.astype(BF16)) + _dot(a.astype(BF16), vv)
        st[...] = jnp.exp(glast) * s_old + _dot_tn(vv, (kk * jnp.exp(glast - gcum)).astype(BF16))
        o = o * lax.rsqrt(jnp.mean(o * o, axis=-1, keepdims=True) + RMS_EPS) * nrm
        y_ref[0, sl, :] = (o * jax.nn.silu(r_ref[0, sl, :])).astype(y_ref.dtype)
        return carry

    lax.fori_loop(0, t // chunk, step, 0)
    s_ref[0, 0] = st[...]


def _gla(h3, wgate, bgate, nrm, s0t, cols, chunk, valid, hk):
    b, t, _ = h3.shape
    hv = nrm.shape[-1]
    col_q, col_k, col_v, col_r, col_low = cols

    def spec(col, width):
        blk = col // width
        return pl.BlockSpec((1, t, width), lambda i, h, blk=blk: (i, 0, blk + h))

    return pl.pallas_call(
        functools.partial(_gla_body, chunk=chunk, valid=valid, hk=hk),
        grid=(b, HEADS),
        in_specs=[spec(col_q, GLA_HK_PAD), spec(col_k, GLA_HK_PAD), spec(col_v, hv), spec(col_r, hv),
                  pl.BlockSpec((1, t, LANES), lambda i, h: (i, 0, col_low // LANES)),
                  pl.BlockSpec((1, LANES, GLA_HK_PAD), lambda i, h: (h, 0, 0)),
                  pl.BlockSpec((1, 1, GLA_HK_PAD), lambda i, h: (h, 0, 0)),
                  pl.BlockSpec((1, hv), lambda i, h: (0, 0)),
                  pl.BlockSpec((1, 1, hv, GLA_HK_PAD), lambda i, h: (i, h, 0, 0))],
        out_specs=[pl.BlockSpec((1, t, hv), lambda i, h: (i, 0, h)),
                   pl.BlockSpec((1, 1, hv, GLA_HK_PAD), lambda i, h: (i, h, 0, 0))],
        out_shape=[jax.ShapeDtypeStruct((b, t, HEADS * hv), BF16),
                   jax.ShapeDtypeStruct((b, HEADS, hv, GLA_HK_PAD), F32)],
        scratch_shapes=[pltpu.VMEM((hv, GLA_HK_PAD), F32)],
        compiler_params=_params(("parallel", "arbitrary")),
        name="gla",
    )(h3, h3, h3, h3, h3, wgate, bgate, nrm.reshape(1, hv), s0t)


def _pad_last(a, width):
    return jnp.pad(a, ((0, 0),) * (a.ndim - 1) + ((0, width - a.shape[-1]),))


def _pad_heads(a, hk):
    lead = a.shape[:-1]
    return _pad_last(a.reshape(lead + (HEADS, hk)), GLA_HK_PAD).reshape(lead + (HEADS * GLA_HK_PAD,))


def _layout(d_model):
    pool_w = 3 * d_model // 8
    attn_w = N_GROUPS * HEADS * HEAD_DIM
    dv = 3 * d_model // 8
    dk = dv // 2
    sections = (pool_w, attn_w, attn_w, attn_w, dk, dk, dv, GLA_RANK, dv, 3 * d_model)
    src = {}
    acc = 0
    for name, w in zip(("u", "aq", "ak", "av", "gq", "gk", "gv", "glow", "gr", "gates"), sections):
        src[name] = (acc, w)
        acc += w
    order = ("u", "aq", "ak", "av", "gv", "gr", "gq", "gk", "glow")
    widths = {"u": pool_w, "aq": attn_w, "ak": attn_w, "av": attn_w, "gv": dv, "gr": dv,
              "gq": HEADS * GLA_HK_PAD, "gk": HEADS * GLA_HK_PAD, "glow": 512}
    dst = {}
    acc = 0
    for name in order:
        dst[name] = acc
        acc += widths[name]
    return src, dst, widths, order, dk // HEADS, dv // HEADS


def _prep_weights(d_model, ffn_w_gate, ffn_w_up, ffn_w_down, w_in, pool_w, gla_w_gate, gla_b_gate,
                  w_br_pool, w_br_attn, w_br_gla, w_out):
    src, dst, widths, order, hk, hv = _layout(d_model)
    parts = []
    for name in order + ("gates",):
        off, w = src[name]
        blk = w_in[:, :, off:off + w]
        if name in ("gq", "gk"):
            blk = _pad_heads(blk, hk)
        elif name == "glow":
            blk = _pad_last(blk, widths[name])
        parts.append(blk)
    depth = w_in.shape[0]
    wgate = jnp.pad(gla_w_gate, ((0, 0), (0, LANES - GLA_RANK), (0, 0)))
    wgate = _pad_heads(wgate, hk).reshape(depth, LANES, HEADS, GLA_HK_PAD).transpose(0, 2, 1, 3)
    bgate = _pad_heads(gla_b_gate, hk).reshape(depth, HEADS, 1, GLA_HK_PAD)
    return dict(
        wg=ffn_w_gate, wu=ffn_w_up,
        wd=ffn_w_down.astype(BF16),
        w_in=jnp.concatenate(parts, axis=2).astype(BF16),
        gate_col=sum(widths[n] for n in order),
        pool_w=pool_w.astype(BF16),
        gla_wgate=wgate.astype(BF16),
        gla_bgate=bgate.astype(F32),
        w_br_pool=w_br_pool.astype(BF16),
        w_br_attn=w_br_attn.astype(BF16),
        w_br_gla=w_br_gla.astype(BF16),
        w_out=w_out.astype(BF16),
    )


def _ffn_block(x, xb, pw, l, i, gain, bias):
    hid = _ffn_hidden(xb, pw["wg"], pw["wu"], (l, i))
    return _matmul_res_ln(hid, pw["wd"], (l, i), x, gain, bias, 0.5, nkh=2)


def _kv_rows(h3, dst, g, rows):
    gw = HEADS * HEAD_DIM
    b = h3.shape[0]
    k = h3[:, rows, dst["ak"] + g * gw: dst["ak"] + (g + 1) * gw]
    v = h3[:, rows, dst["av"] + g * gw: dst["av"] + (g + 1) * gw]
    return jnp.stack([k, v], axis=2).reshape(b, k.shape[1], 2, HEADS, HEAD_DIM)


def _layer(x, l, bsz, t_new, decode, pw, lw, caches):
    m, d_model = x.shape
    src, dst, widths, order, hk, hv = _layout(d_model)
    xb = x.astype(BF16)
    x, xb = _ffn_block(x, xb, pw, l, 0, lw["ln_gain"][0], lw["ln_bias"][0])
    h = _matmul(xb, pw["w_in"], (l,))
    h3 = h.reshape(bsz, t_new, -1)
    pool_c = widths["u"]
    u = h3[:, :, dst["u"]:dst["u"] + pool_c]
    if decode:
        pool_buf, kv_bufs, gla_s = caches
        t_pad = SUBLANES
        p0 = PAST_LEN
        h3p = jnp.pad(h3, ((0, 0), (0, t_pad - t_new), (0, 0)))
        hist = pool_buf
    else:
        t_pad = t_new
        p0 = 0
        h3p = h3
        hist = jnp.zeros((bsz, POOL_KEEP, pool_c), F32)
        gla_s = jnp.zeros((bsz, HEADS, hk, hv), F32)
    u_ext = jnp.concatenate([jnp.zeros((bsz, POOL_HALO - POOL_KEEP, pool_c), F32), hist, u,
                             jnp.zeros((bsz, t_pad - t_new, pool_c), F32)], axis=1)
    y_pool = _pool(u_ext, pw["pool_w"][l], lw["pool_scale"], p0)[:, :t_new]
    new_pool = jnp.concatenate([hist, u], axis=1)[:, -POOL_KEEP:]
    aw = widths["aq"]
    if decode:
        q, kn, vn = (h3p[:, :, dst[n]:dst[n] + aw] for n in ("aq", "ak", "av"))
        flat = [c.reshape(bsz, c.shape[1], 2 * HEADS * HEAD_DIM) for c in kv_bufs]
        y_attn = _attn_decode(q, kn, vn, flat, *lw["decode_tabs"])[:, :t_new]
        new_kv = [jnp.concatenate([c[:, t_new:], _kv_rows(h3, dst, g, slice(None))], axis=1)
                  for g, c in enumerate(kv_bufs)]
    else:
        y_attn = _attn_prompt(h3, lw["band_tabs"], dst["aq"], dst["ak"], dst["av"])
        new_kv = [_kv_rows(h3, dst, g, slice(t_new - min(w, t_new), None))
                  for g, (w, _) in enumerate(DILATED_GROUPS)]
    s0t = jnp.pad(jnp.swapaxes(gla_s.astype(F32), -1, -2), ((0, 0), (0, 0), (0, 0), (0, GLA_HK_PAD - hk)))
    chunk = SUBLANES if decode else math.gcd(t_new, GLA_CHUNK)
    y_gla, st = _gla(h3p, pw["gla_wgate"][l], pw["gla_bgate"][l], lw["gla_norm"], s0t,
                     (dst["gq"], dst["gk"], dst["gv"], dst["gr"], dst["glow"]), chunk,
                     t_new if decode else chunk, hk)
    y_gla = y_gla[:, :t_new]
    new_s = jnp.swapaxes(st[..., :hk], -1, -2)
    merged = _merge(y_pool.reshape(m, -1), y_attn.reshape(m, -1), y_gla.reshape(m, -1),
                    pw["w_br_pool"], pw["w_br_attn"], pw["w_br_gla"], (l,), h, pw["gate_col"])
    x, xb = _matmul_res_ln(merged, pw["w_out"], (l,), x, lw["ln_gain"][1], lw["ln_bias"][1], 1.0, nkh=1)
    x, xb = _ffn_block(x, xb, pw, l, 1, lw["ln_gain"][2], lw["ln_bias"][2])
    return x, new_pool, new_kv, new_s


def _trunk(x3, decode, prepped, small, caches):
    bsz, t_new, d_model = x3.shape
    x = x3.reshape(bsz * t_new, d_model)
    pools, kvs, glas = [], [], []
    for l in range(DEPTH):
        lc = None
        if decode:
            lc = (caches[0][l], [c[l] for c in caches[1]], caches[2][l])
        x, pb, kvb, s = _layer(x, l, bsz, t_new, decode, prepped, small[l], lc)
        pools.append(pb)
        kvs.append(kvb)
        glas.append(s)
    kv_new = [jnp.stack([kv[g] for kv in kvs]) for g in range(N_GROUPS)]
    return x.reshape(bsz, t_new, d_model), jnp.stack(pools), kv_new, jnp.stack(glas)


def kernel(x_prompt, x_sample, cache_pool, cache_kv_w128, cache_kv_w512, cache_kv_w2048, state_gla, rel_bias, ln_gain, ln_bias, ffn_w_gate, ffn_w_up, ffn_w_down, w_in, pool_w, pool_scale, gla_w_gate, gla_b_gate, gla_norm, w_br_pool, w_br_attn, w_br_gla, w_out):
    d_model = x_prompt.shape[-1]
    prepped = _prep_weights(d_model, ffn_w_gate, ffn_w_up, ffn_w_down, w_in, pool_w, gla_w_gate,
                            gla_b_gate, w_br_pool, w_br_attn, w_br_gla, w_out)
    band_tabs = _band_tables(rel_bias)
    decode_tabs = _decode_tables(rel_bias, x_sample.shape[1], SUBLANES)
    small = [dict(ln_gain=ln_gain[l], ln_bias=ln_bias[l], pool_scale=pool_scale[l], gla_norm=gla_norm[l],
                  band_tabs=band_tabs, decode_tabs=decode_tabs) for l in range(DEPTH)]
    y_p, pool_p, kv_p, gla_p = _trunk(x_prompt, False, prepped, small, None)
    y_s, pool_s, kv_s, gla_s = _trunk(x_sample, True, prepped, small,
                                      (cache_pool, [cache_kv_w128, cache_kv_w512, cache_kv_w2048], state_gla))
    return (y_p, y_s, pool_p, kv_p[0], kv_p[1], kv_p[2], gla_p,
            pool_s, kv_s[0], kv_s[1], kv_s[2], gla_s)
```

```python
import functools
import math

import numpy as np
import jax
import jax.numpy as jnp
from jax import lax
from jax.experimental import pallas as pl
from jax.experimental.pallas import tpu as pltpu

DEPTH = 2
PAST_LEN = 16384
POOL_WINDOWS = (2, 4, 8, 16)
POOL_KEEP = max(POOL_WINDOWS) - 1
DILATED_GROUPS = ((128, 1), (512, 4), (2048, 16))
N_GROUPS = len(DILATED_GROUPS)
HEADS = 4
HEAD_DIM = 128
NUM_BUCKETS = 32
MAX_DISTANCE = 2048
GLA_RANK = 16
GLA_TAU = 16.0
GLA_CHUNK = 64
ALPHA = (2 * DEPTH) ** 0.25
LN_EPS = 1e-5
RMS_EPS = 1e-6
NEG_INF = -1e30

LANES = 128
SUBLANES = 8
MXU_DIM = 256
VMEM_LIMIT = 60 * 1024 * 1024

BF16 = jnp.bfloat16
F32 = jnp.float32

GLA_HK_PAD = 256
ATTN_BLOCK = 128
POOL_HALO = 16


def _params(sem):
    return pltpu.CompilerParams(dimension_semantics=sem, vmem_limit_bytes=VMEM_LIMIT)


def _dot(a, b):
    return jnp.dot(a, b, preferred_element_type=F32)


def _dot_nt(a, b):
    return lax.dot_general(a, b, (((1,), (1,)), ((), ())), preferred_element_type=F32)


def _dot_tn(a, b):
    return lax.dot_general(a, b, (((0,), (0,)), ((), ())), preferred_element_type=F32)


def _row_tile(m, pref):
    return pref if m % pref == 0 else m


def _wspec(prefix, rows, cols, index):
    return pl.BlockSpec((None,) * len(prefix) + (rows, cols), lambda *g: tuple(prefix) + tuple(index(*g)))


def _gateup_body(x_ref, wg_ref, wu_ref, o_ref):
    x = x_ref[...]
    g = _dot(x, wg_ref[...].astype(BF16))
    u = _dot(x, wu_ref[...].astype(BF16))
    o_ref[...] = (jax.nn.silu(g) * u).astype(o_ref.dtype)


def _resident_rows(tm, d):
    return pl.BlockSpec((tm, d), lambda i, j: (i, 0), pipeline_mode=pl.Buffered(1))


def _ffn_hidden(xb, wg, wu, prefix, tm_pref=2048, tn=256):
    m, d = xb.shape
    f = wg.shape[-1]
    tm = _row_tile(m, tm_pref)
    wspec = _wspec(prefix, d, tn, lambda i, j: (0, j))
    return pl.pallas_call(
        _gateup_body,
        grid=(m // tm, f // tn),
        in_specs=[_resident_rows(tm, d), wspec, wspec],
        out_specs=pl.BlockSpec((tm, tn), lambda i, j: (i, j)),
        out_shape=jax.ShapeDtypeStruct((m, f), BF16),
        compiler_params=_params(("parallel", "arbitrary")),
        name="ffn_hidden",
    )(xb, wg, wu)


def _plain_body(x_ref, w_ref, o_ref):
    o_ref[...] = _dot(x_ref[...], w_ref[...]).astype(o_ref.dtype)


def _matmul(xb, w, prefix, tm_pref=2048, tn=512, out_dtype=F32):
    m, d = xb.shape
    n = w.shape[-1]
    tm = _row_tile(m, tm_pref)
    return pl.pallas_call(
        _plain_body,
        grid=(m // tm, n // tn),
        in_specs=[_resident_rows(tm, d),
                  _wspec(prefix, d, tn, lambda i, j: (0, j))],
        out_specs=pl.BlockSpec((tm, tn), lambda i, j: (i, j)),
        out_shape=jax.ShapeDtypeStruct((m, n), out_dtype),
        compiler_params=_params(("parallel", "arbitrary")),
        name="matmul",
    )(xb, w)


LN_ROWS = 64


def _res_ln_body(a_ref, w_ref, x_ref, g_ref, b_ref, of_ref, ob_ref, *, coef, nkh, nj, tn):
    kh = pl.program_id(1)
    j = pl.program_id(2)
    col = pl.ds(pl.multiple_of(j * tn, tn), tn)
    part = _dot(a_ref[...], w_ref[...])
    if nkh == 1:
        of_ref[:, col] = ALPHA * x_ref[...] + coef * part
    else:
        @pl.when(kh == 0)
        def _():
            of_ref[:, col] = part

        @pl.when((kh > 0) & (kh < nkh - 1))
        def _():
            of_ref[:, col] += part

        @pl.when(kh == nkh - 1)
        def _():
            of_ref[:, col] = ALPHA * x_ref[...] + coef * (of_ref[:, col] + part)

    @pl.when((kh == nkh - 1) & (j == nj - 1))
    def _():
        gain = g_ref[...]
        bias = b_ref[...]
        ln_rows = math.gcd(of_ref.shape[0], LN_ROWS)

        def rows(r, carry):
            sl = pl.ds(pl.multiple_of(r * ln_rows, ln_rows), ln_rows)
            mu = jnp.mean(of_ref[sl, :], axis=-1, keepdims=True)
            c = of_ref[sl, :] - mu
            rstd = lax.rsqrt(jnp.mean(c * c, axis=-1, keepdims=True) + LN_EPS)
            y = (of_ref[sl, :] - mu) * rstd * gain + bias
            of_ref[sl, :] = y
            ob_ref[sl, :] = y.astype(ob_ref.dtype)
            return carry

        lax.fori_loop(0, of_ref.shape[0] // ln_rows, rows, 0)


def _matmul_res_ln(a, w, prefix, x, gain, bias, coef, nkh, tm_pref=512, tn=512):
    m, kdim = a.shape
    d = w.shape[-1]
    tm = _row_tile(m, tm_pref)
    tk = kdim // nkh
    nj = d // tn
    return pl.pallas_call(
        functools.partial(_res_ln_body, coef=coef, nkh=nkh, nj=nj, tn=tn),
        grid=(m // tm, nkh, nj),
        in_specs=[pl.BlockSpec((tm, tk), lambda i, k, j: (i, k)),
                  _wspec(prefix, tk, tn, lambda i, k, j: (k, j)),
                  pl.BlockSpec((tm, tn), lambda i, k, j: (i, jnp.where(k == nkh - 1, j, 0))),
                  pl.BlockSpec((1, d), lambda i, k, j: (0, 0)),
                  pl.BlockSpec((1, d), lambda i, k, j: (0, 0))],
        out_specs=[pl.BlockSpec((tm, d), lambda i, k, j: (i, 0)),
                   pl.BlockSpec((tm, d), lambda i, k, j: (i, 0))],
        out_shape=[jax.ShapeDtypeStruct((m, d), F32), jax.ShapeDtypeStruct((m, d), BF16)],
        compiler_params=_params(("parallel", "arbitrary", "arbitrary")),
        name="matmul_res_ln",
    )(a, w, x, gain.reshape(1, d), bias.reshape(1, d))


def _merge_body(yp_ref, ya_ref, yg_ref, wp_ref, wa_ref, wg_ref, gp_ref, ga_ref, gg_ref, o_ref):
    acc = jax.nn.sigmoid(gp_ref[...]) * _dot(yp_ref[...], wp_ref[...])
    acc += jax.nn.sigmoid(ga_ref[...]) * _dot(ya_ref[...], wa_ref[...])
    acc += jax.nn.sigmoid(gg_ref[...]) * _dot(yg_ref[...], wg_ref[...])
    o_ref[...] = acc.astype(o_ref.dtype)


def _merge(y_pool, y_attn, y_gla, w_pool, w_attn, w_gla, prefix, h, gate_col, tm_pref=1024, tn=512):
    m = y_pool.shape[0]
    d = w_pool.shape[-1]
    tm = _row_tile(m, tm_pref)
    nd = d // tn
    g0 = gate_col // tn

    def yspec(y):
        return pl.BlockSpec((tm, y.shape[1]), lambda i, j: (i, 0))

    def wspec(w):
        return _wspec(prefix, w.shape[-2], tn, lambda i, j: (0, j))

    def gspec(n):
        return pl.BlockSpec((tm, tn), lambda i, j: (i, g0 + n * nd + j))

    return pl.pallas_call(
        _merge_body,
        grid=(m // tm, nd),
        in_specs=[yspec(y_pool), yspec(y_attn), yspec(y_gla), wspec(w_pool), wspec(w_attn), wspec(w_gla),
                  gspec(0), gspec(1), gspec(2)],
        out_specs=pl.BlockSpec((tm, tn), lambda i, j: (i, j)),
        out_shape=jax.ShapeDtypeStruct((m, d), BF16),
        compiler_params=_params(("parallel", "arbitrary")),
        name="merge",
    )(y_pool, y_attn, y_gla, w_pool, w_attn, w_gla, h, h, h)


def _pool_body(hist_ref, u_ref, w_ref, s_ref, o_ref, ext, *, p0, rows):
    t = o_ref.shape[1]
    gi = pl.program_id(1)
    ext[pl.ds(0, POOL_HALO), :] = hist_ref[0]
    ext[pl.ds(POOL_HALO, t), :] = u_ref[0]
    for idx, win in enumerate(POOL_WINDOWS):
        @pl.when(gi == idx)
        def _(win=win):
            wmat = w_ref[0]
            scale = s_ref[0]
            for r0 in range(0, t, rows):
                cur = ext[pl.ds(POOL_HALO + r0, rows), :]
                tot = cur
                for s in range(1, win):
                    tot = tot + ext[pl.ds(POOL_HALO + r0 - s, rows), :]
                pos = p0 + r0 + lax.broadcasted_iota(jnp.int32, (rows, 1), 0)
                cnt = jnp.minimum(pos + 1, win).astype(F32)
                dlt = tot / cnt - cur
                y = _dot(dlt.astype(BF16), wmat) * scale
                o_ref[0, pl.ds(r0, rows), :] = y.astype(o_ref.dtype)


def _pool(h3, col_u, hist, w_grp, scale, p0):
    b, t, _ = h3.shape
    c = hist.shape[-1]
    g = len(POOL_WINDOWS)
    gc = c // g
    rows = min(t, 256)
    return pl.pallas_call(
        functools.partial(_pool_body, p0=p0, rows=rows),
        grid=(b, g),
        in_specs=[pl.BlockSpec((1, POOL_HALO, gc), lambda i, j: (i, 0, j)),
                  pl.BlockSpec((1, t, gc), lambda i, j: (i, 0, col_u // gc + j)),
                  pl.BlockSpec((1, gc, gc), lambda i, j: (j, 0, 0)),
                  pl.BlockSpec((1, 1, gc), lambda i, j: (j, 0, 0))],
        out_specs=pl.BlockSpec((1, t, gc), lambda i, j: (i, 0, j)),
        out_shape=jax.ShapeDtypeStruct((b, t, c), BF16),
        scratch_shapes=[pltpu.VMEM((POOL_HALO + t, gc), F32)],
        compiler_params=_params(("parallel", "arbitrary")),
        name="pool",
    )(hist, h3, w_grp, scale.reshape(g, 1, gc))


def _t5_bias(rel_bias, g):
    w, d = DILATED_GROUPS[g]
    dist = d * np.arange(w // d + 1)
    max_exact = NUM_BUCKETS // 2
    df = jnp.maximum(dist, 1).astype(F32)
    large = max_exact + (jnp.log(df / max_exact) / math.log(MAX_DISTANCE / max_exact)
                         * (NUM_BUCKETS - max_exact)).astype(jnp.int32)
    bucket = jnp.where(dist < max_exact, dist, jnp.minimum(large, NUM_BUCKETS - 1))
    return rel_bias[bucket][:, g * HEADS:(g + 1) * HEADS].T.astype(F32)


def _neg(rows, n):
    return jnp.full((rows, n), NEG_INF, F32)


def _toeplitz(p):
    h, n = p.shape
    b = n // 2
    return jnp.tile(p, (1, b))[:, :b * (n - 1)].reshape(h, b, n - 1)[:, :, :b]


def _band_tables(rel_bias):
    blk = ATTN_BLOCK
    tabs = []
    for g in range(N_GROUPS):
        bias = _t5_bias(rel_bias, g)
        cur = _toeplitz(jnp.concatenate([bias[:, :1], _neg(HEADS, blk), jnp.flip(bias[:, 1:blk], 1)], 1))
        prev = _toeplitz(jnp.concatenate([jnp.flip(bias[:, 1:blk + 1], 1), _neg(HEADS, blk)], 1))
        tabs.append(jnp.stack([cur, prev], axis=1))
    return jnp.stack(tabs)


def _softmax_block(scores, values):
    m = scores[0].max(axis=-1, keepdims=True)
    for s in scores[1:]:
        m = jnp.maximum(m, s.max(axis=-1, keepdims=True))
    l = None
    o = None
    for s, v in zip(scores, values):
        p = jnp.exp(s - m)
        ls = p.sum(axis=-1, keepdims=True)
        os_ = _dot(p.astype(BF16), v.astype(BF16))
        l = ls if l is None else l + ls
        o = os_ if o is None else o + os_
    return m, l, o


def _attn_prompt_body(q0, k0, v0, q1, k1, v1, q2, k2, v2, tab_ref, o_ref, oacc, macc, lacc, *, t):
    qkv = ((q0, k0, v0), (q1, k1, v1), (q2, k2, v2))
    scale = HEAD_DIM ** -0.5
    for g, (_, d) in enumerate(DILATED_GROUPS):
        q_ref, k_ref, v_ref = qkv[g]
        span = ATTN_BLOCK * d

        def rows(start, ref):
            if d == 1:
                return ref[0, pl.ds(start, ATTN_BLOCK), :]
            return ref[0, pl.ds(start, ATTN_BLOCK, stride=d), :]

        for s in range(t // span):
            for r in range(d):
                start = s * span + r
                qb = rows(start, q_ref).astype(BF16)
                scores = [_dot_nt(qb, rows(start, k_ref).astype(BF16)) * scale + tab_ref[g, 0, 0]]
                values = [rows(start, v_ref)]
                if s > 0:
                    scores.append(_dot_nt(qb, rows(start - span, k_ref).astype(BF16)) * scale + tab_ref[g, 0, 1])
                    values.append(rows(start - span, v_ref))
                m, l, o = _softmax_block(scores, values)
                if d == 1:
                    sl = pl.ds(start, ATTN_BLOCK)
                else:
                    sl = pl.ds(start, ATTN_BLOCK, stride=d)
                oacc[g, sl, :] = o
                macc[g, sl, :] = jnp.broadcast_to(m, (ATTN_BLOCK, LANES))
                lacc[g, sl, :] = jnp.broadcast_to(l, (ATTN_BLOCK, LANES))
    for r0 in range(0, t, ATTN_BLOCK):
        sl = pl.ds(r0, ATTN_BLOCK)
        ms = [macc[g, sl, :] for g in range(N_GROUPS)]
        mm = jnp.maximum(jnp.maximum(ms[0], ms[1]), ms[2])
        den = None
        num = None
        for g in range(N_GROUPS):
            c = jnp.exp(ms[g] - mm)
            dn = lacc[g, sl, :] * c
            nm = oacc[g, sl, :] * c
            den = dn if den is None else den + dn
            num = nm if num is None else num + nm
        o_ref[0, sl, :] = (num / den).astype(o_ref.dtype)


def _attn_prompt(h3, tabs, col_q, col_k, col_v):
    b, t, _ = h3.shape
    assert t % (ATTN_BLOCK * max(d for _, d in DILATED_GROUPS)) == 0

    def spec(col, g):
        blk = col // HEAD_DIM + HEADS * g
        return pl.BlockSpec((1, t, HEAD_DIM), lambda i, h, blk=blk: (i, 0, blk + h))

    in_specs = []
    for g in range(N_GROUPS):
        in_specs += [spec(col_q, g), spec(col_k, g), spec(col_v, g)]
    in_specs.append(pl.BlockSpec((N_GROUPS, 1, 2, ATTN_BLOCK, ATTN_BLOCK), lambda i, h: (0, h, 0, 0, 0)))
    return pl.pallas_call(
        functools.partial(_attn_prompt_body, t=t),
        grid=(b, HEADS),
        in_specs=in_specs,
        out_specs=pl.BlockSpec((1, t, HEAD_DIM), lambda i, h: (i, 0, h)),
        out_shape=jax.ShapeDtypeStruct((b, t, HEADS * HEAD_DIM), BF16),
        scratch_shapes=[pltpu.VMEM((N_GROUPS, t, HEAD_DIM), F32),
                        pltpu.VMEM((N_GROUPS, t, LANES), F32),
                        pltpu.VMEM((N_GROUPS, t, LANES), F32)],
        compiler_params=_params(("parallel", "arbitrary")),
        name="attn_prompt",
    )(*([h3] * 9), tabs)


def _decode_tables(rel_bias, t_new, t_pad):
    tabs_c, tabs_n = [], []
    for g, (w, d) in enumerate(DILATED_GROUPS):
        bias = _t5_bias(rel_bias, g)
        by_dist = jnp.pad(bias[:, :, None], ((0, 0), (0, 0), (0, d - 1)), constant_values=NEG_INF)
        by_dist = by_dist.reshape(HEADS, -1)[:, :w + 1]
        rows_c, rows_n = [], []
        for t in range(t_pad):
            if t < t_new:
                rows_c.append(jnp.flip(jnp.concatenate([by_dist[:, t + 1:], _neg(HEADS, t)], 1), 1))
                rows_n.append(jnp.concatenate([jnp.flip(by_dist[:, :t + 1], 1), _neg(HEADS, t_pad - t - 1)], 1))
            else:
                rows_c.append(_neg(HEADS, w))
                rows_n.append(_neg(HEADS, t_pad))
        tabs_c.append(jnp.stack(rows_c, axis=1))
        tabs_n.append(jnp.stack(rows_n, axis=1))
    return tabs_c, tabs_n


def _attn_decode_body(q_ref, kn_ref, vn_ref, c0, c1, c2, tc0, tc1, tc2, tn0, tn1, tn2, o_ref):
    caches = (c0, c1, c2)
    tcs = (tc0, tc1, tc2)
    tns = (tn0, tn1, tn2)
    scale = HEAD_DIM ** -0.5
    per_pos = 2 * HEADS
    for h in range(HEADS):
        stats = []
        for g in range(N_GROUPS):
            col = pl.ds((g * HEADS + h) * HEAD_DIM, HEAD_DIM)
            qb = q_ref[0, :, col].astype(BF16)
            w = caches[g].shape[1] // per_pos
            kc = caches[g][0, pl.ds(h, w, stride=per_pos), :]
            vc = caches[g][0, pl.ds(HEADS + h, w, stride=per_pos), :]
            scores = [_dot_nt(qb, kc.astype(BF16)) * scale + tcs[g][h],
                      _dot_nt(qb, kn_ref[0, :, col].astype(BF16)) * scale + tns[g][h]]
            stats.append(_softmax_block(scores, [vc, vn_ref[0, :, col]]))
        mm = jnp.maximum(jnp.maximum(stats[0][0], stats[1][0]), stats[2][0])
        den = None
        num = None
        for m, l, o in stats:
            c = jnp.exp(m - mm)
            den = l * c if den is None else den + l * c
            num = o * c if num is None else num + o * c
        o_ref[0, :, pl.ds(h * HEAD_DIM, HEAD_DIM)] = (num / den).astype(o_ref.dtype)


def _attn_decode(q, kn, vn, caches, l, tabs_c, tabs_n):
    b, tp, _ = q.shape

    def full(a):
        nd = a.ndim
        return pl.BlockSpec(a.shape, lambda i, nd=nd: (0,) * nd)

    def per_batch(a):
        return pl.BlockSpec((1,) + a.shape[1:], lambda i: (i, 0, 0))

    def cache(a):
        return pl.BlockSpec((None, 1) + a.shape[2:], lambda i: (l, i, 0, 0))

    args = [q, kn, vn, *caches, *tabs_c, *tabs_n]
    in_specs = ([per_batch(a) for a in args[:3]] + [cache(a) for a in args[3:6]]
                + [full(a) for a in args[6:]])
    return pl.pallas_call(
        _attn_decode_body,
        grid=(b,),
        in_specs=in_specs,
        out_specs=pl.BlockSpec((1, tp, HEADS * HEAD_DIM), lambda i: (i, 0, 0)),
        out_shape=jax.ShapeDtypeStruct((b, tp, HEADS * HEAD_DIM), BF16),
        compiler_params=_params(("parallel",)),
        name="attn_decode",
    )(*args)


def _gla_body(q_ref, k_ref, v_ref, r_ref, low_ref, wg_ref, bg_ref, nrm_ref, s0_ref, y_ref, s_ref, st,
              *, chunk, valid, hk, hv):
    ti = pl.program_id(1)
    tt = q_ref.shape[1]

    @pl.when(ti == 0)
    def _():
        st[...] = s0_ref[0]

    wgate = wg_ref[...]
    bgate = bg_ref[...]
    nrm = nrm_ref[...]
    row = lax.broadcasted_iota(jnp.int32, (chunk, chunk), 0)
    colm = lax.broadcasted_iota(jnp.int32, (chunk, chunk), 1)
    causal = row >= colm
    tri = causal.astype(F32)
    live = lax.broadcasted_iota(jnp.int32, (chunk, 1), 0) < valid

    def step(c, carry):
        sl = pl.ds(pl.multiple_of(c * chunk, chunk), chunk)
        z = _dot(low_ref[0, sl, :].astype(BF16), wgate) + bgate
        glog = jnp.where(live, jax.nn.log_sigmoid(z) / GLA_TAU, 0.0)
        gcum = jnp.dot(tri, glog, precision=lax.Precision.HIGHEST, preferred_element_type=F32)
        glast = gcum[chunk - 1:chunk, :]
        kk = k_ref[0, sl, :]
        qg = ((q_ref[0, sl, :] * (hk ** -0.5)) * jnp.exp(gcum)).astype(BF16)
        kdn = (kk * jnp.exp(-gcum)).astype(BF16)
        kdec = (kk * jnp.exp(glast - gcum)).astype(BF16)
        decay = jnp.exp(glast)
        for h in range(HEADS):
            ks = slice(h * GLA_HK_PAD, (h + 1) * GLA_HK_PAD)
            vs = pl.ds(h * hv, hv)
            vv = v_ref[0, sl, vs].astype(BF16)
            a = jnp.where(causal, _dot_nt(qg[:, ks], kdn[:, ks]), 0.0)
            s_old = st[h]
            o = _dot_nt(qg[:, ks], s_old.astype(BF16)) + _dot(a.astype(BF16), vv)
            st[h] = decay[:, ks] * s_old + _dot_tn(vv, kdec[:, ks])
            o = o * lax.rsqrt(jnp.mean(o * o, axis=-1, keepdims=True) + RMS_EPS) * nrm
            y_ref[0, sl, vs] = (o * jax.nn.silu(r_ref[0, sl, vs])).astype(y_ref.dtype)
        return carry

    lax.fori_loop(0, tt // chunk, step, 0)

    @pl.when(ti == pl.num_programs(1) - 1)
    def _():
        s_ref[0] = st[...]


GLA_TIME_TILE = 512


def _gla(h3, wgate, bgate, nrm, s0t, cols, chunk, valid, hk):
    b, t, _ = h3.shape
    hv = nrm.shape[-1]
    col_q, col_k, col_v, col_r, col_low = cols
    tt = _row_tile(t, GLA_TIME_TILE)
    kw = HEADS * GLA_HK_PAD
    vw = HEADS * hv

    def spec(col, width):
        return pl.BlockSpec((1, tt, width), lambda i, j: (i, j, col // width))

    state = pl.BlockSpec((1, HEADS, hv, GLA_HK_PAD), lambda i, j: (i, 0, 0, 0))
    return pl.pallas_call(
        functools.partial(_gla_body, chunk=chunk, valid=valid, hk=hk, hv=hv),
        grid=(b, t // tt),
        in_specs=[spec(col_q, kw), spec(col_k, kw), spec(col_v, vw), spec(col_r, vw), spec(col_low, LANES),
                  pl.BlockSpec((LANES, kw), lambda i, j: (0, 0)),
                  pl.BlockSpec((1, kw), lambda i, j: (0, 0)),
                  pl.BlockSpec((1, hv), lambda i, j: (0, 0)),
                  state],
        out_specs=[pl.BlockSpec((1, tt, vw), lambda i, j: (i, j, 0)), state],
        out_shape=[jax.ShapeDtypeStruct((b, t, vw), BF16),
                   jax.ShapeDtypeStruct((b, HEADS, hv, GLA_HK_PAD), F32)],
        scratch_shapes=[pltpu.VMEM((HEADS, hv, GLA_HK_PAD), F32)],
        compiler_params=_params(("parallel", "arbitrary")),
        name="gla",
    )(h3, h3, h3, h3, h3, wgate, bgate, nrm.reshape(1, hv), s0t)


def _pad_last(a, width):
    return jnp.pad(a, ((0, 0),) * (a.ndim - 1) + ((0, width - a.shape[-1]),))


def _pad_heads(a, hk):
    lead = a.shape[:-1]
    return _pad_last(a.reshape(lead + (HEADS, hk)), GLA_HK_PAD).reshape(lead + (HEADS * GLA_HK_PAD,))


def _layout(d_model):
    pool_w = 3 * d_model // 8
    attn_w = N_GROUPS * HEADS * HEAD_DIM
    dv = 3 * d_model // 8
    dk = dv // 2
    sections = (pool_w, attn_w, attn_w, attn_w, dk, dk, dv, GLA_RANK, dv, 3 * d_model)
    src = {}
    acc = 0
    for name, w in zip(("u", "aq", "ak", "av", "gq", "gk", "gv", "glow", "gr", "gates"), sections):
        src[name] = (acc, w)
        acc += w
    order = ("u", "aq", "ak", "av", "gv", "gr", "gq", "gk", "glow")
    widths = {"u": pool_w, "aq": attn_w, "ak": attn_w, "av": attn_w, "gv": dv, "gr": dv,
              "gq": HEADS * GLA_HK_PAD, "gk": HEADS * GLA_HK_PAD, "glow": 512}
    dst = {}
    acc = 0
    for name in order:
        dst[name] = acc
        acc += widths[name]
    return src, dst, widths, order, dk // HEADS, dv // HEADS


def _prep_weights(d_model, ffn_w_gate, ffn_w_up, ffn_w_down, w_in, pool_w, gla_w_gate, gla_b_gate,
                  w_br_pool, w_br_attn, w_br_gla, w_out):
    src, dst, widths, order, hk, hv = _layout(d_model)
    parts = []
    for name in order + ("gates",):
        off, w = src[name]
        blk = w_in[:, :, off:off + w]
        if name in ("gq", "gk"):
            blk = _pad_heads(blk, hk)
        elif name == "glow":
            blk = _pad_last(blk, widths[name])
        parts.append(blk)
    wgate = _pad_heads(jnp.pad(gla_w_gate, ((0, 0), (0, LANES - GLA_RANK), (0, 0))), hk)
    bgate = _pad_heads(gla_b_gate, hk)[:, None, :]
    return dict(
        wg=ffn_w_gate, wu=ffn_w_up,
        wd=ffn_w_down.astype(BF16),
        w_in=jnp.concatenate(parts, axis=2).astype(BF16),
        gate_col=sum(widths[n] for n in order),
        pool_w=pool_w.astype(BF16),
        gla_wgate=wgate.astype(BF16),
        gla_bgate=bgate.astype(F32),
        w_br_pool=w_br_pool.astype(BF16),
        w_br_attn=w_br_attn.astype(BF16),
        w_br_gla=w_br_gla.astype(BF16),
        w_out=w_out.astype(BF16),
    )


def _ffn_block(x, xb, pw, l, i, gain, bias):
    hid = _ffn_hidden(xb, pw["wg"], pw["wu"], (l, i))
    return _matmul_res_ln(hid, pw["wd"], (l, i), x, gain, bias, 0.5, nkh=2)


def _kv_rows(h3, dst, g, rows):
    gw = HEADS * HEAD_DIM
    b = h3.shape[0]
    k = h3[:, rows, dst["ak"] + g * gw: dst["ak"] + (g + 1) * gw]
    v = h3[:, rows, dst["av"] + g * gw: dst["av"] + (g + 1) * gw]
    return jnp.stack([k, v], axis=2).reshape(b, k.shape[1], 2, HEADS, HEAD_DIM)


def _layer(x, xb, l, bsz, t_new, decode, pw, lw, caches):
    m, d_model = x.shape
    src, dst, widths, order, hk, hv = _layout(d_model)
    x, xb = _ffn_block(x, xb, pw, l, 0, lw["ln_gain"][0], lw["ln_bias"][0])
    h = _matmul(xb, pw["w_in"], (l,))
    h3 = h.reshape(bsz, t_new, -1)
    pool_c = widths["u"]
    u = h3[:, :, dst["u"]:dst["u"] + pool_c]
    if decode:
        pool_buf, kv_bufs, gla_s = caches
        t_pad = SUBLANES
        p0 = PAST_LEN
        h3p = jnp.pad(h3, ((0, 0), (0, t_pad - t_new), (0, 0)))
        hist = pool_buf
    else:
        t_pad = t_new
        p0 = 0
        h3p = h3
        hist = jnp.zeros((bsz, POOL_KEEP, pool_c), F32)
        gla_s = jnp.zeros((bsz, HEADS, hk, hv), F32)
    hist_ext = jnp.pad(hist, ((0, 0), (POOL_HALO - POOL_KEEP, 0), (0, 0)))
    y_pool = _pool(h3p, dst["u"], hist_ext, pw["pool_w"][l], lw["pool_scale"], p0)[:, :t_new]
    new_pool = jnp.concatenate([hist, u], axis=1)[:, -POOL_KEEP:]
    aw = widths["aq"]
    if decode:
        q, kn, vn = (h3p[:, :, dst[n]:dst[n] + aw] for n in ("aq", "ak", "av"))
        flat = [c.reshape(c.shape[0], bsz, c.shape[2] * 2 * HEADS, HEAD_DIM) for c in kv_bufs]
        y_attn = _attn_decode(q, kn, vn, flat, l, *lw["decode_tabs"])[:, :t_new]
        new_kv = [jnp.concatenate([c[l, :, t_new:], _kv_rows(h3, dst, g, slice(None))], axis=1)
                  for g, c in enumerate(kv_bufs)]
    else:
        y_attn = _attn_prompt(h3, lw["band_tabs"], dst["aq"], dst["ak"], dst["av"])
        new_kv = [_kv_rows(h3, dst, g, slice(t_new - min(w, t_new), None))
                  for g, (w, _) in enumerate(DILATED_GROUPS)]
    s0t = jnp.pad(jnp.swapaxes(gla_s.astype(F32), -1, -2), ((0, 0), (0, 0), (0, 0), (0, GLA_HK_PAD - hk)))
    chunk = SUBLANES if decode else math.gcd(t_new, GLA_CHUNK)
    y_gla, st = _gla(h3p, pw["gla_wgate"][l], pw["gla_bgate"][l], lw["gla_norm"], s0t,
                     (dst["gq"], dst["gk"], dst["gv"], dst["gr"], dst["glow"]), chunk,
                     t_new if decode else chunk, hk)
    y_gla = y_gla[:, :t_new]
    new_s = jnp.swapaxes(st[..., :hk], -1, -2)
    merged = _merge(y_pool.reshape(m, -1), y_attn.reshape(m, -1), y_gla.reshape(m, -1),
                    pw["w_br_pool"], pw["w_br_attn"], pw["w_br_gla"], (l,), h, pw["gate_col"])
    x, xb = _matmul_res_ln(merged, pw["w_out"], (l,), x, lw["ln_gain"][1], lw["ln_bias"][1], 1.0, nkh=1)
    x, xb = _ffn_block(x, xb, pw, l, 1, lw["ln_gain"][2], lw["ln_bias"][2])
    return x, xb, new_pool, new_kv, new_s


def _trunk(x3, decode, prepped, small, caches):
    bsz, t_new, d_model = x3.shape
    x = x3.reshape(bsz * t_new, d_model)
    xb = x.astype(BF16)
    pools, kvs, glas = [], [], []
    for l in range(DEPTH):
        lc = None
        if decode:
            lc = (caches[0][l], caches[1], caches[2][l])
        x, xb, pb, kvb, s = _layer(x, xb, l, bsz, t_new, decode, prepped, small[l], lc)
        pools.append(pb)
        kvs.append(kvb)
        glas.append(s)
    kv_new = [jnp.stack([kv[g] for kv in kvs]) for g in range(N_GROUPS)]
    return x.reshape(bsz, t_new, d_model), jnp.stack(pools), kv_new, jnp.stack(glas)


def kernel(x_prompt, x_sample, cache_pool, cache_kv_w128, cache_kv_w512, cache_kv_w2048, state_gla, rel_bias, ln_gain, ln_bias, ffn_w_gate, ffn_w_up, ffn_w_down, w_in, pool_w, pool_scale, gla_w_gate, gla_b_gate, gla_norm, w_br_pool, w_br_attn, w_br_gla, w_out):
    d_model = x_prompt.shape[-1]
    prepped = _prep_weights(d_model, ffn_w_gate, ffn_w_up, ffn_w_down, w_in, pool_w, gla_w_gate,
                            gla_b_gate, w_br_pool, w_br_attn, w_br_gla, w_out)
    band_tabs = _band_tables(rel_bias)
    decode_tabs = _decode_tables(rel_bias, x_sample.shape[1], SUBLANES)
    small = [dict(ln_gain=ln_gain[l], ln_bias=ln_bias[l], pool_scale=pool_scale[l], gla_norm=gla_norm[l],
                  band_tabs=band_tabs, decode_tabs=decode_tabs) for l in range(DEPTH)]
    y_p, pool_p, kv_p, gla_p = _trunk(x_prompt, False, prepped, small, None)
    y_s, pool_s, kv_s, gla_s = _trunk(x_sample, True, prepped, small,
                                      (cache_pool, [cache_kv_w128, cache_kv_w512, cache_kv_w2048], state_gla))
    return (y_p, y_s, pool_p, kv_p[0], kv_p[1], kv_p[2], gla_p,
            pool_s, kv_s[0], kv_s[1], kv_s[2], gla_s)
```

```python
import functools
import math

import numpy as np
import jax
import jax.numpy as jnp
from jax import lax
from jax.experimental import pallas as pl
from jax.experimental.pallas import tpu as pltpu

DEPTH = 2
PAST_LEN = 16384
POOL_WINDOWS = (2, 4, 8, 16)
POOL_KEEP = max(POOL_WINDOWS) - 1
DILATED_GROUPS = ((128, 1), (512, 4), (2048, 16))
N_GROUPS = len(DILATED_GROUPS)
HEADS = 4
HEAD_DIM = 128
NUM_BUCKETS = 32
MAX_DISTANCE = 2048
GLA_RANK = 16
GLA_TAU = 16.0
GLA_CHUNK = 64
ALPHA = (2 * DEPTH) ** 0.25
LN_EPS = 1e-5
RMS_EPS = 1e-6
NEG_INF = -1e30

LANES = 128
SUBLANES = 8
MXU_DIM = 256
VMEM_LIMIT = 60 * 1024 * 1024

BF16 = jnp.bfloat16
F32 = jnp.float32

GLA_HK_PAD = 256
ATTN_BLOCK = 128
POOL_HALO = 16


def _params(sem):
    return pltpu.CompilerParams(dimension_semantics=sem, vmem_limit_bytes=VMEM_LIMIT)


def _dot(a, b):
    return jnp.dot(a, b, preferred_element_type=F32)


def _dot_nt(a, b):
    return lax.dot_general(a, b, (((1,), (1,)), ((), ())), preferred_element_type=F32)


def _dot_tn(a, b):
    return lax.dot_general(a, b, (((0,), (0,)), ((), ())), preferred_element_type=F32)


def _row_tile(m, pref):
    return pref if m % pref == 0 else m


def _wspec(prefix, rows, cols, index):
    return pl.BlockSpec((None,) * len(prefix) + (rows, cols), lambda *g: tuple(prefix) + tuple(index(*g)))


def _side_spec(a):
    nd = len(a.shape)
    return pl.BlockSpec(tuple(a.shape), lambda *g: (0,) * nd)


def _col(j, tn):
    return pl.ds(pl.multiple_of(j * tn, tn), tn)


def _gateup_body(x_ref, xs_ref, wg_ref, wu_ref, o_ref, os_ref, *, tn):
    wg = wg_ref[...].astype(BF16)
    wu = wu_ref[...].astype(BF16)
    x = x_ref[...]
    o_ref[...] = (jax.nn.silu(_dot(x, wg)) * _dot(x, wu)).astype(o_ref.dtype)

    @pl.when(pl.program_id(0) == 0)
    def _():
        xs = xs_ref[...]
        os_ref[:, _col(pl.program_id(1), tn)] = (jax.nn.silu(_dot(xs, wg)) * _dot(xs, wu)).astype(os_ref.dtype)


def _resident_rows(tm, d):
    return pl.BlockSpec((tm, d), lambda i, j: (i, 0), pipeline_mode=pl.Buffered(1))


def _ffn_hidden(xb, xsb, wg, wu, prefix, tm_pref=2048, tn=256):
    m, d = xb.shape
    f = wg.shape[-1]
    tm = _row_tile(m, tm_pref)
    wspec = _wspec(prefix, d, tn, lambda i, j: (0, j))
    side_out = jax.ShapeDtypeStruct((xsb.shape[0], f), BF16)
    return pl.pallas_call(
        functools.partial(_gateup_body, tn=tn),
        grid=(m // tm, f // tn),
        in_specs=[_resident_rows(tm, d), _side_spec(xsb), wspec, wspec],
        out_specs=[pl.BlockSpec((tm, tn), lambda i, j: (i, j)), _side_spec(side_out)],
        out_shape=[jax.ShapeDtypeStruct((m, f), BF16), side_out],
        compiler_params=_params(("arbitrary", "arbitrary")),
        name="ffn_hidden",
    )(xb, xsb, wg, wu)


def _plain_body(x_ref, xs_ref, w_ref, o_ref, os_ref, *, tn):
    w = w_ref[...]
    o_ref[...] = _dot(x_ref[...], w).astype(o_ref.dtype)

    @pl.when(pl.program_id(0) == 0)
    def _():
        os_ref[:, _col(pl.program_id(1), tn)] = _dot(xs_ref[...], w).astype(os_ref.dtype)


def _matmul(xb, xsb, w, prefix, tm_pref=2048, tn=512, out_dtype=F32):
    m, d = xb.shape
    n = w.shape[-1]
    tm = _row_tile(m, tm_pref)
    side_out = jax.ShapeDtypeStruct((xsb.shape[0], n), out_dtype)
    return pl.pallas_call(
        functools.partial(_plain_body, tn=tn),
        grid=(m // tm, n // tn),
        in_specs=[_resident_rows(tm, d), _side_spec(xsb),
                  _wspec(prefix, d, tn, lambda i, j: (0, j))],
        out_specs=[pl.BlockSpec((tm, tn), lambda i, j: (i, j)), _side_spec(side_out)],
        out_shape=[jax.ShapeDtypeStruct((m, n), out_dtype), side_out],
        compiler_params=_params(("arbitrary", "arbitrary")),
        name="matmul",
    )(xb, xsb, w)


LN_ROWS = 64


def _res_accumulate(of_ref, col, part, x_tile, kh, nkh, coef):
    if nkh == 1:
        of_ref[:, col] = ALPHA * x_tile() + coef * part
        return

    @pl.when(kh == 0)
    def _():
        of_ref[:, col] = part

    @pl.when((kh > 0) & (kh < nkh - 1))
    def _():
        of_ref[:, col] += part

    @pl.when(kh == nkh - 1)
    def _():
        of_ref[:, col] = ALPHA * x_tile() + coef * (of_ref[:, col] + part)


def _layer_norm_rows(of_ref, ob_ref, gain, bias):
    ln_rows = math.gcd(of_ref.shape[0], LN_ROWS)

    def rows(r, carry):
        sl = pl.ds(pl.multiple_of(r * ln_rows, ln_rows), ln_rows)
        mu = jnp.mean(of_ref[sl, :], axis=-1, keepdims=True)
        c = of_ref[sl, :] - mu
        rstd = lax.rsqrt(jnp.mean(c * c, axis=-1, keepdims=True) + LN_EPS)
        y = (of_ref[sl, :] - mu) * rstd * gain + bias
        of_ref[sl, :] = y
        ob_ref[sl, :] = y.astype(ob_ref.dtype)
        return carry

    lax.fori_loop(0, of_ref.shape[0] // ln_rows, rows, 0)


def _res_ln_body(a_ref, as_ref, w_ref, x_ref, xs_ref, g_ref, b_ref, of_ref, ob_ref, ofs_ref, obs_ref,
                 *, coef, nkh, nj, tn, tk):
    i = pl.program_id(0)
    kh = pl.program_id(1)
    j = pl.program_id(2)
    col = _col(j, tn)
    w = w_ref[...]
    last = (kh == nkh - 1) & (j == nj - 1)
    _res_accumulate(of_ref, col, _dot(a_ref[...], w), lambda: x_ref[...], kh, nkh, coef)

    @pl.when(last)
    def _():
        _layer_norm_rows(of_ref, ob_ref, g_ref[...], b_ref[...])

    @pl.when(i == 0)
    def _():
        part = _dot(as_ref[:, pl.ds(pl.multiple_of(kh * tk, LANES), tk)], w)
        _res_accumulate(ofs_ref, col, part, lambda: xs_ref[:, col], kh, nkh, coef)

    @pl.when((i == 0) & last)
    def _():
        _layer_norm_rows(ofs_ref, obs_ref, g_ref[...], b_ref[...])


def _matmul_res_ln(a, a_s, w, prefix, x, x_s, gain, bias, coef, nkh, tm_pref=512, tn=512):
    m, kdim = a.shape
    d = w.shape[-1]
    tm = _row_tile(m, tm_pref)
    tk = kdim // nkh
    nj = d // tn
    ms = a_s.shape[0]
    side_f = jax.ShapeDtypeStruct((ms, d), F32)
    side_b = jax.ShapeDtypeStruct((ms, d), BF16)
    vec = pl.BlockSpec((1, d), lambda i, k, j: (0, 0))
    rows = pl.BlockSpec((tm, d), lambda i, k, j: (i, 0))
    return pl.pallas_call(
        functools.partial(_res_ln_body, coef=coef, nkh=nkh, nj=nj, tn=tn, tk=tk),
        grid=(m // tm, nkh, nj),
        in_specs=[pl.BlockSpec((tm, tk), lambda i, k, j: (i, k)), _side_spec(a_s),
                  _wspec(prefix, tk, tn, lambda i, k, j: (k, j)),
                  pl.BlockSpec((tm, tn), lambda i, k, j: (i, jnp.where(k == nkh - 1, j, 0))),
                  _side_spec(x_s), vec, vec],
        out_specs=[rows, rows, _side_spec(side_f), _side_spec(side_b)],
        out_shape=[jax.ShapeDtypeStruct((m, d), F32), jax.ShapeDtypeStruct((m, d), BF16), side_f, side_b],
        compiler_params=_params(("arbitrary", "arbitrary", "arbitrary")),
        name="matmul_res_ln",
    )(a, a_s, w, x, x_s, gain.reshape(1, d), bias.reshape(1, d))


def _merge_body(yp_ref, ya_ref, yg_ref, sp_ref, sa_ref, sg_ref, wp_ref, wa_ref, wg_ref,
                gp_ref, ga_ref, gg_ref, hs_ref, o_ref, os_ref, *, tn, g0, nd):
    wp = wp_ref[...]
    wa = wa_ref[...]
    wg = wg_ref[...]
    acc = jax.nn.sigmoid(gp_ref[...]) * _dot(yp_ref[...], wp)
    acc += jax.nn.sigmoid(ga_ref[...]) * _dot(ya_ref[...], wa)
    acc += jax.nn.sigmoid(gg_ref[...]) * _dot(yg_ref[...], wg)
    o_ref[...] = acc.astype(o_ref.dtype)

    @pl.when(pl.program_id(0) == 0)
    def _():
        j = pl.program_id(1)
        acc_s = None
        for n, (y_ref, w) in enumerate(((sp_ref, wp), (sa_ref, wa), (sg_ref, wg))):
            term = jax.nn.sigmoid(hs_ref[:, _col(g0 + n * nd + j, tn)]) * _dot(y_ref[...], w)
            acc_s = term if acc_s is None else acc_s + term
        os_ref[:, _col(j, tn)] = acc_s.astype(os_ref.dtype)


def _merge(ys, ys_side, ws, prefix, h, h_side, gate_col, tm_pref=1024, tn=512):
    m = ys[0].shape[0]
    d = ws[0].shape[-1]
    tm = _row_tile(m, tm_pref)
    nd = d // tn
    g0 = gate_col // tn
    side_out = jax.ShapeDtypeStruct((h_side.shape[0], d), BF16)

    def yspec(y):
        return pl.BlockSpec((tm, y.shape[1]), lambda i, j: (i, 0))

    def wspec(w):
        return _wspec(prefix, w.shape[-2], tn, lambda i, j: (0, j))

    def gspec(n):
        return pl.BlockSpec((tm, tn), lambda i, j: (i, g0 + n * nd + j))

    return pl.pallas_call(
        functools.partial(_merge_body, tn=tn, g0=g0, nd=nd),
        grid=(m // tm, nd),
        in_specs=([yspec(y) for y in ys] + [_side_spec(y) for y in ys_side] + [wspec(w) for w in ws]
                  + [gspec(0), gspec(1), gspec(2), _side_spec(h_side)]),
        out_specs=[pl.BlockSpec((tm, tn), lambda i, j: (i, j)), _side_spec(side_out)],
        out_shape=[jax.ShapeDtypeStruct((m, d), BF16), side_out],
        compiler_params=_params(("arbitrary", "arbitrary")),
        name="merge",
    )(*ys, *ys_side, *ws, h, h, h, h_side)


def _pool_body(hist_ref, u_ref, w_ref, s_ref, o_ref, ext, *, p0, rows):
    t = o_ref.shape[1]
    gi = pl.program_id(1)
    ext[pl.ds(0, POOL_HALO), :] = hist_ref[0]
    ext[pl.ds(POOL_HALO, t), :] = u_ref[0]
    for idx, win in enumerate(POOL_WINDOWS):
        @pl.when(gi == idx)
        def _(win=win):
            wmat = w_ref[0]
            scale = s_ref[0]
            for r0 in range(0, t, rows):
                cur = ext[pl.ds(POOL_HALO + r0, rows), :]
                tot = cur
                for s in range(1, win):
                    tot = tot + ext[pl.ds(POOL_HALO + r0 - s, rows), :]
                pos = p0 + r0 + lax.broadcasted_iota(jnp.int32, (rows, 1), 0)
                cnt = jnp.minimum(pos + 1, win).astype(F32)
                dlt = tot / cnt - cur
                y = _dot(dlt.astype(BF16), wmat) * scale
                o_ref[0, pl.ds(r0, rows), :] = y.astype(o_ref.dtype)


def _pool(h3, col_u, hist, w_grp, scale, p0):
    b, t, _ = h3.shape
    c = hist.shape[-1]
    g = len(POOL_WINDOWS)
    gc = c // g
    rows = min(t, 256)
    return pl.pallas_call(
        functools.partial(_pool_body, p0=p0, rows=rows),
        grid=(b, g),
        in_specs=[pl.BlockSpec((1, POOL_HALO, gc), lambda i, j: (i, 0, j)),
                  pl.BlockSpec((1, t, gc), lambda i, j: (i, 0, col_u // gc + j)),
                  pl.BlockSpec((1, gc, gc), lambda i, j: (j, 0, 0)),
                  pl.BlockSpec((1, 1, gc), lambda i, j: (j, 0, 0))],
        out_specs=pl.BlockSpec((1, t, gc), lambda i, j: (i, 0, j)),
        out_shape=jax.ShapeDtypeStruct((b, t, c), BF16),
        scratch_shapes=[pltpu.VMEM((POOL_HALO + t, gc), F32)],
        compiler_params=_params(("parallel", "arbitrary")),
        name="pool",
    )(hist, h3, w_grp, scale.reshape(g, 1, gc))


def _t5_bias(rel_bias, g):
    w, d = DILATED_GROUPS[g]
    dist = d * np.arange(w // d + 1)
    max_exact = NUM_BUCKETS // 2
    df = jnp.maximum(dist, 1).astype(F32)
    large = max_exact + (jnp.log(df / max_exact) / math.log(MAX_DISTANCE / max_exact)
                         * (NUM_BUCKETS - max_exact)).astype(jnp.int32)
    bucket = jnp.where(dist < max_exact, dist, jnp.minimum(large, NUM_BUCKETS - 1))
    return rel_bias[bucket][:, g * HEADS:(g + 1) * HEADS].T.astype(F32)


def _neg(rows, n):
    return jnp.full((rows, n), NEG_INF, F32)


def _toeplitz(p):
    h, n = p.shape
    b = n // 2
    return jnp.tile(p, (1, b))[:, :b * (n - 1)].reshape(h, b, n - 1)[:, :, :b]


def _band_tables(rel_bias):
    blk = ATTN_BLOCK
    tabs = []
    for g in range(N_GROUPS):
        bias = _t5_bias(rel_bias, g)
        cur = _toeplitz(jnp.concatenate([bias[:, :1], _neg(HEADS, blk), jnp.flip(bias[:, 1:blk], 1)], 1))
        prev = _toeplitz(jnp.concatenate([jnp.flip(bias[:, 1:blk + 1], 1), _neg(HEADS, blk)], 1))
        tabs.append(jnp.stack([cur, prev], axis=1))
    return jnp.stack(tabs)


def _softmax_block(scores, values):
    same = all(s.shape == scores[0].shape for s in scores)
    if same:
        m = functools.reduce(jnp.maximum, scores).max(axis=-1, keepdims=True)
    else:
        m = functools.reduce(jnp.maximum, [s.max(axis=-1, keepdims=True) for s in scores])
    ps = [jnp.exp(s - m) for s in scores]
    if same:
        l = functools.reduce(jnp.add, ps).sum(axis=-1, keepdims=True)
    else:
        l = functools.reduce(jnp.add, [p.sum(axis=-1, keepdims=True) for p in ps])
    o = functools.reduce(jnp.add, [_dot(p.astype(BF16), v.astype(BF16)) for p, v in zip(ps, values)])
    return m, l, o


def _attn_prompt_body(q0, k0, v0, q1, k1, v1, q2, k2, v2, tab_ref, o_ref, oacc, macc, lacc, *, t):
    qkv = ((q0, k0, v0), (q1, k1, v1), (q2, k2, v2))
    scale = HEAD_DIM ** -0.5
    for g, (_, d) in enumerate(DILATED_GROUPS):
        q_ref, k_ref, v_ref = qkv[g]
        span = ATTN_BLOCK * d

        def rows(start, ref):
            if d == 1:
                return ref[0, pl.ds(start, ATTN_BLOCK), :]
            return ref[0, pl.ds(start, ATTN_BLOCK, stride=d), :]

        for s in range(t // span):
            for r in range(d):
                start = s * span + r
                qb = rows(start, q_ref).astype(BF16)
                scores = [_dot_nt(qb, rows(start, k_ref).astype(BF16)) * scale + tab_ref[g, 0, 0]]
                values = [rows(start, v_ref)]
                if s > 0:
                    scores.append(_dot_nt(qb, rows(start - span, k_ref).astype(BF16)) * scale + tab_ref[g, 0, 1])
                    values.append(rows(start - span, v_ref))
                m, l, o = _softmax_block(scores, values)
                if d == 1:
                    sl = pl.ds(start, ATTN_BLOCK)
                else:
                    sl = pl.ds(start, ATTN_BLOCK, stride=d)
                oacc[g, sl, :] = o
                macc[g, sl, :] = jnp.broadcast_to(m, (ATTN_BLOCK, LANES))
                lacc[g, sl, :] = jnp.broadcast_to(l, (ATTN_BLOCK, LANES))
    for r0 in range(0, t, ATTN_BLOCK):
        sl = pl.ds(r0, ATTN_BLOCK)
        ms = [macc[g, sl, :] for g in range(N_GROUPS)]
        mm = jnp.maximum(jnp.maximum(ms[0], ms[1]), ms[2])
        den = None
        num = None
        for g in range(N_GROUPS):
            c = jnp.exp(ms[g] - mm)
            dn = lacc[g, sl, :] * c
            nm = oacc[g, sl, :] * c
            den = dn if den is None else den + dn
            num = nm if num is None else num + nm
        o_ref[0, sl, :] = (num / den).astype(o_ref.dtype)


def _attn_prompt(h3, tabs, col_q, col_k, col_v):
    b, t, _ = h3.shape
    assert t % (ATTN_BLOCK * max(d for _, d in DILATED_GROUPS)) == 0

    def spec(col, g):
        blk = col // HEAD_DIM + HEADS * g
        return pl.BlockSpec((1, t, HEAD_DIM), lambda i, h, blk=blk: (i, 0, blk + h))

    in_specs = []
    for g in range(N_GROUPS):
        in_specs += [spec(col_q, g), spec(col_k, g), spec(col_v, g)]
    in_specs.append(pl.BlockSpec((N_GROUPS, 1, 2, ATTN_BLOCK, ATTN_BLOCK), lambda i, h: (0, h, 0, 0, 0)))
    return pl.pallas_call(
        functools.partial(_attn_prompt_body, t=t),
        grid=(b, HEADS),
        in_specs=in_specs,
        out_specs=pl.BlockSpec((1, t, HEAD_DIM), lambda i, h: (i, 0, h)),
        out_shape=jax.ShapeDtypeStruct((b, t, HEADS * HEAD_DIM), BF16),
        scratch_shapes=[pltpu.VMEM((N_GROUPS, t, HEAD_DIM), F32),
                        pltpu.VMEM((N_GROUPS, t, LANES), F32),
                        pltpu.VMEM((N_GROUPS, t, LANES), F32)],
        compiler_params=_params(("parallel", "arbitrary")),
        name="attn_prompt",
    )(*([h3] * 9), tabs)


def _decode_tables(rel_bias, t_new, t_pad):
    tabs_c, tabs_n = [], []
    for g, (w, d) in enumerate(DILATED_GROUPS):
        bias = _t5_bias(rel_bias, g)
        by_dist = jnp.pad(bias[:, :, None], ((0, 0), (0, 0), (0, d - 1)), constant_values=NEG_INF)
        by_dist = by_dist.reshape(HEADS, -1)[:, :w + 1]
        rows_c, rows_n = [], []
        for t in range(t_pad):
            if t < t_new:
                rows_c.append(jnp.flip(jnp.concatenate([by_dist[:, t + 1:], _neg(HEADS, t)], 1), 1))
                rows_n.append(jnp.concatenate([jnp.flip(by_dist[:, :t + 1], 1), _neg(HEADS, t_pad - t - 1)], 1))
            else:
                rows_c.append(_neg(HEADS, w))
                rows_n.append(_neg(HEADS, t_pad))
        tabs_c.append(jnp.stack(rows_c, axis=1))
        tabs_n.append(jnp.stack(rows_n, axis=1))
    return tabs_c, tabs_n


def _attn_decode_body(q_ref, kn_ref, vn_ref, c0, c1, c2, tc0, tc1, tc2, tn0, tn1, tn2, o_ref):
    caches = (c0, c1, c2)
    tcs = (tc0, tc1, tc2)
    tns = (tn0, tn1, tn2)
    scale = HEAD_DIM ** -0.5
    per_pos = 2 * HEADS
    for h in range(HEADS):
        stats = []
        for g in range(N_GROUPS):
            col = pl.ds((g * HEADS + h) * HEAD_DIM, HEAD_DIM)
            qb = q_ref[0, :, col].astype(BF16)
            w = caches[g].shape[1] // per_pos
            kc = caches[g][0, pl.ds(h, w, stride=per_pos), :]
            vc = caches[g][0, pl.ds(HEADS + h, w, stride=per_pos), :]
            scores = [_dot_nt(qb, kc.astype(BF16)) * scale + tcs[g][h],
                      _dot_nt(qb, kn_ref[0, :, col].astype(BF16)) * scale + tns[g][h]]
            stats.append(_softmax_block(scores, [vc, vn_ref[0, :, col]]))
        mm = jnp.maximum(jnp.maximum(stats[0][0], stats[1][0]), stats[2][0])
        den = None
        num = None
        for m, l, o in stats:
            c = jnp.exp(m - mm)
            den = l * c if den is None else den + l * c
            num = o * c if num is None else num + o * c
        o_ref[0, :, pl.ds(h * HEAD_DIM, HEAD_DIM)] = (num / den).astype(o_ref.dtype)


def _attn_decode(q, kn, vn, caches, l, tabs_c, tabs_n):
    b, tp, _ = q.shape

    def full(a):
        nd = a.ndim
        return pl.BlockSpec(a.shape, lambda i, nd=nd: (0,) * nd)

    def per_batch(a):
        return pl.BlockSpec((1,) + a.shape[1:], lambda i: (i, 0, 0))

    def cache(a):
        return pl.BlockSpec((None, 1) + a.shape[2:], lambda i: (l, i, 0, 0))

    args = [q, kn, vn, *caches, *tabs_c, *tabs_n]
    in_specs = ([per_batch(a) for a in args[:3]] + [cache(a) for a in args[3:6]]
                + [full(a) for a in args[6:]])
    return pl.pallas_call(
        _attn_decode_body,
        grid=(b,),
        in_specs=in_specs,
        out_specs=pl.BlockSpec((1, tp, HEADS * HEAD_DIM), lambda i: (i, 0, 0)),
        out_shape=jax.ShapeDtypeStruct((b, tp, HEADS * HEAD_DIM), BF16),
        compiler_params=_params(("parallel",)),
        name="attn_decode",
    )(*args)


def _gla_body(q_ref, k_ref, v_ref, r_ref, low_ref, wg_ref, bg_ref, nrm_ref, s0_ref, y_ref, s_ref, st,
              *, chunk, valid, hk, hv):
    ti = pl.program_id(1)
    tt = q_ref.shape[1]

    @pl.when(ti == 0)
    def _():
        st[...] = s0_ref[0]

    wgate = wg_ref[...]
    bgate = bg_ref[...]
    nrm = nrm_ref[...]
    row = lax.broadcasted_iota(jnp.int32, (chunk, chunk), 0)
    colm = lax.broadcasted_iota(jnp.int32, (chunk, chunk), 1)
    causal = row >= colm
    tri = causal.astype(F32)
    live = lax.broadcasted_iota(jnp.int32, (chunk, 1), 0) < valid

    def step(c, carry):
        sl = pl.ds(pl.multiple_of(c * chunk, chunk), chunk)
        z = _dot(low_ref[0, sl, :].astype(BF16), wgate) + bgate
        glog = jnp.where(live, jax.nn.log_sigmoid(z) / GLA_TAU, 0.0)
        gcum = jnp.dot(tri, glog, precision=lax.Precision.HIGHEST, preferred_element_type=F32)
        glast = gcum[chunk - 1:chunk, :]
        kk = k_ref[0, sl, :]
        qg = ((q_ref[0, sl, :] * (hk ** -0.5)) * jnp.exp(gcum)).astype(BF16)
        kdn = (kk * jnp.exp(-gcum)).astype(BF16)
        kdec = (kk * jnp.exp(glast - gcum)).astype(BF16)
        decay = jnp.exp(glast)
        for h in range(HEADS):
            ks = slice(h * GLA_HK_PAD, (h + 1) * GLA_HK_PAD)
            vs = pl.ds(h * hv, hv)
            vv = v_ref[0, sl, vs].astype(BF16)
            a = jnp.where(causal, _dot_nt(qg[:, ks], kdn[:, ks]), 0.0)
            s_old = st[h]
            o = _dot_nt(qg[:, ks], s_old.astype(BF16)) + _dot(a.astype(BF16), vv)
            st[h] = decay[:, ks] * s_old + _dot_tn(vv, kdec[:, ks])
            o = o * lax.rsqrt(jnp.mean(o * o, axis=-1, keepdims=True) + RMS_EPS) * nrm
            y_ref[0, sl, vs] = (o * jax.nn.silu(r_ref[0, sl, vs])).astype(y_ref.dtype)
        return carry

    lax.fori_loop(0, tt // chunk, step, 0, unroll=min(2, tt // chunk))

    @pl.when(ti == pl.num_programs(1) - 1)
    def _():
        s_ref[0] = st[...]


GLA_TIME_TILE = 512


def _gla(h3, wgate, bgate, nrm, s0t, cols, chunk, valid, hk):
    b, t, _ = h3.shape
    hv = nrm.shape[-1]
    col_q, col_k, col_v, col_r, col_low = cols
    tt = _row_tile(t, GLA_TIME_TILE)
    kw = HEADS * GLA_HK_PAD
    vw = HEADS * hv

    def spec(col, width):
        return pl.BlockSpec((1, tt, width), lambda i, j: (i, j, col // width))

    state = pl.BlockSpec((1, HEADS, hv, GLA_HK_PAD), lambda i, j: (i, 0, 0, 0))
    return pl.pallas_call(
        functools.partial(_gla_body, chunk=chunk, valid=valid, hk=hk, hv=hv),
        grid=(b, t // tt),
        in_specs=[spec(col_q, kw), spec(col_k, kw), spec(col_v, vw), spec(col_r, vw), spec(col_low, LANES),
                  pl.BlockSpec((LANES, kw), lambda i, j: (0, 0)),
                  pl.BlockSpec((1, kw), lambda i, j: (0, 0)),
                  pl.BlockSpec((1, hv), lambda i, j: (0, 0)),
                  state],
        out_specs=[pl.BlockSpec((1, tt, vw), lambda i, j: (i, j, 0)), state],
        out_shape=[jax.ShapeDtypeStruct((b, t, vw), BF16),
                   jax.ShapeDtypeStruct((b, HEADS, hv, GLA_HK_PAD), F32)],
        scratch_shapes=[pltpu.VMEM((HEADS, hv, GLA_HK_PAD), F32)],
        compiler_params=_params(("parallel", "arbitrary")),
        name="gla",
    )(h3, h3, h3, h3, h3, wgate, bgate, nrm.reshape(1, hv), s0t)


def _pad_last(a, width):
    return jnp.pad(a, ((0, 0),) * (a.ndim - 1) + ((0, width - a.shape[-1]),))


def _pad_heads(a, hk):
    lead = a.shape[:-1]
    return _pad_last(a.reshape(lead + (HEADS, hk)), GLA_HK_PAD).reshape(lead + (HEADS * GLA_HK_PAD,))


def _layout(d_model):
    pool_w = 3 * d_model // 8
    attn_w = N_GROUPS * HEADS * HEAD_DIM
    dv = 3 * d_model // 8
    dk = dv // 2
    sections = (pool_w, attn_w, attn_w, attn_w, dk, dk, dv, GLA_RANK, dv, 3 * d_model)
    src = {}
    acc = 0
    for name, w in zip(("u", "aq", "ak", "av", "gq", "gk", "gv", "glow", "gr", "gates"), sections):
        src[name] = (acc, w)
        acc += w
    order = ("u", "aq", "ak", "av", "gv", "gr", "gq", "gk", "glow")
    widths = {"u": pool_w, "aq": attn_w, "ak": attn_w, "av": attn_w, "gv": dv, "gr": dv,
              "gq": HEADS * GLA_HK_PAD, "gk": HEADS * GLA_HK_PAD, "glow": 512}
    dst = {}
    acc = 0
    for name in order:
        dst[name] = acc
        acc += widths[name]
    return src, dst, widths, order, dk // HEADS, dv // HEADS


def _prep_weights(d_model, ffn_w_gate, ffn_w_up, ffn_w_down, w_in, pool_w, gla_w_gate, gla_b_gate,
                  w_br_pool, w_br_attn, w_br_gla, w_out):
    src, dst, widths, order, hk, hv = _layout(d_model)
    parts = []
    for name in order + ("gates",):
        off, w = src[name]
        blk = w_in[:, :, off:off + w]
        if name in ("gq", "gk"):
            blk = _pad_heads(blk, hk)
        elif name == "glow":
            blk = _pad_last(blk, widths[name])
        parts.append(blk)
    wgate = _pad_heads(jnp.pad(gla_w_gate, ((0, 0), (0, LANES - GLA_RANK), (0, 0))), hk)
    bgate = _pad_heads(gla_b_gate, hk)[:, None, :]
    return dict(
        wg=ffn_w_gate, wu=ffn_w_up,
        wd=ffn_w_down.astype(BF16),
        w_in=jnp.concatenate(parts, axis=2).astype(BF16),
        gate_col=sum(widths[n] for n in order),
        pool_w=pool_w.astype(BF16),
        gla_wgate=wgate.astype(BF16),
        gla_bgate=bgate.astype(F32),
        w_br_pool=w_br_pool.astype(BF16),
        w_br_attn=w_br_attn.astype(BF16),
        w_br_gla=w_br_gla.astype(BF16),
        w_out=w_out.astype(BF16),
    )


def _ffn_block(xp, xs, pw, l, i, gain, bias):
    hid_p, hid_s = _ffn_hidden(xp[1], xs[1], pw["wg"], pw["wu"], (l, i))
    of, ob, ofs, obs = _matmul_res_ln(hid_p, hid_s, pw["wd"], (l, i), xp[0], xs[0], gain, bias, 0.5, nkh=2)
    return (of, ob), (ofs, obs)


def _kv_rows(h3, dst, g, rows):
    gw = HEADS * HEAD_DIM
    b = h3.shape[0]
    k = h3[:, rows, dst["ak"] + g * gw: dst["ak"] + (g + 1) * gw]
    v = h3[:, rows, dst["av"] + g * gw: dst["av"] + (g + 1) * gw]
    return jnp.stack([k, v], axis=2).reshape(b, k.shape[1], 2, HEADS, HEAD_DIM)


def _mixers(h, l, bsz, t_new, decode, pw, lw, caches):
    m = h.shape[0]
    src, dst, widths, order, hk, hv = _layout(lw["ln_gain"].shape[-1])
    h3 = h.reshape(bsz, t_new, -1)
    pool_c = widths["u"]
    u = h3[:, :, dst["u"]:dst["u"] + pool_c]
    if decode:
        pool_buf, kv_bufs, gla_s = caches
        t_pad = SUBLANES
        p0 = PAST_LEN
        h3p = jnp.pad(h3, ((0, 0), (0, t_pad - t_new), (0, 0)))
        hist = pool_buf
    else:
        t_pad = t_new
        p0 = 0
        h3p = h3
        hist = jnp.zeros((bsz, POOL_KEEP, pool_c), F32)
        gla_s = jnp.zeros((bsz, HEADS, hk, hv), F32)
    hist_ext = jnp.pad(hist, ((0, 0), (POOL_HALO - POOL_KEEP, 0), (0, 0)))
    y_pool = _pool(h3p, dst["u"], hist_ext, pw["pool_w"][l], lw["pool_scale"], p0)[:, :t_new]
    new_pool = jnp.concatenate([hist, u], axis=1)[:, -POOL_KEEP:]
    aw = widths["aq"]
    if decode:
        q, kn, vn = (h3p[:, :, dst[n]:dst[n] + aw] for n in ("aq", "ak", "av"))
        flat = [c.reshape(c.shape[0], bsz, c.shape[2] * 2 * HEADS, HEAD_DIM) for c in kv_bufs]
        y_attn = _attn_decode(q, kn, vn, flat, l, *lw["decode_tabs"])[:, :t_new]
        new_kv = [jnp.concatenate([c[l, :, t_new:], _kv_rows(h3, dst, g, slice(None))], axis=1)
                  for g, c in enumerate(kv_bufs)]
    else:
        y_attn = _attn_prompt(h3, lw["band_tabs"], dst["aq"], dst["ak"], dst["av"])
        new_kv = [_kv_rows(h3, dst, g, slice(t_new - min(w, t_new), None))
                  for g, (w, _) in enumerate(DILATED_GROUPS)]
    s0t = jnp.pad(jnp.swapaxes(gla_s.astype(F32), -1, -2), ((0, 0), (0, 0), (0, 0), (0, GLA_HK_PAD - hk)))
    chunk = SUBLANES if decode else math.gcd(t_new, GLA_CHUNK)
    y_gla, st = _gla(h3p, pw["gla_wgate"][l], pw["gla_bgate"][l], lw["gla_norm"], s0t,
                     (dst["gq"], dst["gk"], dst["gv"], dst["gr"], dst["glow"]), chunk,
                     t_new if decode else chunk, hk)
    y_gla = y_gla[:, :t_new]
    new_s = jnp.swapaxes(st[..., :hk], -1, -2)
    ys = [y.reshape(m, -1) for y in (y_pool, y_attn, y_gla)]
    return ys, new_pool, new_kv, new_s


def _layer(xp, xs, l, shape_p, shape_s, pw, lw, caches):
    xp, xs = _ffn_block(xp, xs, pw, l, 0, lw["ln_gain"][0], lw["ln_bias"][0])
    h_p, h_s = _matmul(xp[1], xs[1], pw["w_in"], (l,))
    ys_p, *new_p = _mixers(h_p, l, *shape_p, False, pw, lw, None)
    ys_s, *new_s = _mixers(h_s, l, *shape_s, True, pw, lw, caches)
    merged_p, merged_s = _merge(ys_p, ys_s, [pw["w_br_pool"], pw["w_br_attn"], pw["w_br_gla"]], (l,),
                                h_p, h_s, pw["gate_col"])
    of, ob, ofs, obs = _matmul_res_ln(merged_p, merged_s, pw["w_out"], (l,), xp[0], xs[0],
                                      lw["ln_gain"][1], lw["ln_bias"][1], 1.0, nkh=1)
    xp, xs = _ffn_block((of, ob), (ofs, obs), pw, l, 1, lw["ln_gain"][2], lw["ln_bias"][2])
    return xp, xs, new_p, new_s


def kernel(x_prompt, x_sample, cache_pool, cache_kv_w128, cache_kv_w512, cache_kv_w2048, state_gla, rel_bias, ln_gain, ln_bias, ffn_w_gate, ffn_w_up, ffn_w_down, w_in, pool_w, pool_scale, gla_w_gate, gla_b_gate, gla_norm, w_br_pool, w_br_attn, w_br_gla, w_out):
    d_model = x_prompt.shape[-1]
    pw = _prep_weights(d_model, ffn_w_gate, ffn_w_up, ffn_w_down, w_in, pool_w, gla_w_gate,
                       gla_b_gate, w_br_pool, w_br_attn, w_br_gla, w_out)
    band_tabs = _band_tables(rel_bias)
    decode_tabs = _decode_tables(rel_bias, x_sample.shape[1], SUBLANES)
    kv_caches = [cache_kv_w128, cache_kv_w512, cache_kv_w2048]
    shape_p = x_prompt.shape[:2]
    shape_s = x_sample.shape[:2]
    xp = x_prompt.reshape(-1, d_model)
    xs = x_sample.reshape(-1, d_model)
    xp = (xp, xp.astype(BF16))
    xs = (xs, xs.astype(BF16))
    outs_p, outs_s = [], []
    for l in range(DEPTH):
        lw = dict(ln_gain=ln_gain[l], ln_bias=ln_bias[l], pool_scale=pool_scale[l], gla_norm=gla_norm[l],
                  band_tabs=band_tabs, decode_tabs=decode_tabs)
        xp, xs, new_p, new_s = _layer(xp, xs, l, shape_p, shape_s, pw, lw,
                                      (cache_pool[l], kv_caches, state_gla[l]))
        outs_p.append(new_p)
        outs_s.append(new_s)

    def stacked(outs):
        pools = jnp.stack([o[0] for o in outs])
        kvs = [jnp.stack([o[1][g] for o in outs]) for g in range(N_GROUPS)]
        return pools, kvs, jnp.stack([o[2] for o in outs])

    pool_p, kv_p, gla_p = stacked(outs_p)
    pool_s, kv_s, gla_s = stacked(outs_s)
    return (xp[0].reshape(x_prompt.shape), xs[0].reshape(x_sample.shape),
            pool_p, kv_p[0], kv_p[1], kv_p[2], gla_p,
            pool_s, kv_s[0], kv_s[1], kv_s[2], gla_s)
```

```python
import functools
import math

import numpy as np
import jax
import jax.numpy as jnp
from jax import lax
from jax.experimental import pallas as pl
from jax.experimental.pallas import tpu as pltpu

DEPTH = 2
PAST_LEN = 16384
POOL_WINDOWS = (2, 4, 8, 16)
POOL_KEEP = max(POOL_WINDOWS) - 1
DILATED_GROUPS = ((128, 1), (512, 4), (2048, 16))
N_GROUPS = len(DILATED_GROUPS)
HEADS = 4
HEAD_DIM = 128
NUM_BUCKETS = 32
MAX_DISTANCE = 2048
GLA_RANK = 16
GLA_TAU = 16.0
GLA_CHUNK = 64
ALPHA = (2 * DEPTH) ** 0.25
LN_EPS = 1e-5
RMS_EPS = 1e-6
NEG_INF = -1e30

LANES = 128
SUBLANES = 8
MXU_DIM = 256
VMEM_LIMIT = 60 * 1024 * 1024

BF16 = jnp.bfloat16
F32 = jnp.float32

GLA_HK_PAD = 256
ATTN_BLOCK = 128
POOL_HALO = 16


def _params(sem):
    return pltpu.CompilerParams(dimension_semantics=sem, vmem_limit_bytes=VMEM_LIMIT)


def _dot(a, b):
    return jnp.dot(a, b, preferred_element_type=F32)


def _dot_nt(a, b):
    return lax.dot_general(a, b, (((1,), (1,)), ((), ())), preferred_element_type=F32)


def _dot_tn(a, b):
    return lax.dot_general(a, b, (((0,), (0,)), ((), ())), preferred_element_type=F32)


def _row_tile(m, pref):
    return pref if m % pref == 0 else m


def _wspec(prefix, rows, cols, index):
    return pl.BlockSpec((None,) * len(prefix) + (rows, cols), lambda *g: tuple(prefix) + tuple(index(*g)))


def _side_spec(a):
    nd = len(a.shape)
    return pl.BlockSpec(tuple(a.shape), lambda *g: (0,) * nd)


def _col(j, tn):
    return pl.ds(pl.multiple_of(j * tn, tn), tn)


def _gateup_body(x_ref, xs_ref, wg_ref, wu_ref, o_ref, os_ref, *, tn):
    wg = wg_ref[...].astype(BF16)
    wu = wu_ref[...].astype(BF16)
    x = x_ref[...]
    o_ref[...] = (jax.nn.silu(_dot(x, wg)) * _dot(x, wu)).astype(o_ref.dtype)

    @pl.when(pl.program_id(0) == 0)
    def _():
        xs = xs_ref[...]
        os_ref[:, _col(pl.program_id(1), tn)] = (jax.nn.silu(_dot(xs, wg)) * _dot(xs, wu)).astype(os_ref.dtype)


def _resident_rows(tm, d):
    return pl.BlockSpec((tm, d), lambda i, j: (i, 0), pipeline_mode=pl.Buffered(1))


def _ffn_hidden(xb, xsb, wg, wu, prefix, tm_pref=2048, tn=256):
    m, d = xb.shape
    f = wg.shape[-1]
    tm = _row_tile(m, tm_pref)
    wspec = _wspec(prefix, d, tn, lambda i, j: (0, j))
    side_out = jax.ShapeDtypeStruct((xsb.shape[0], f), BF16)
    return pl.pallas_call(
        functools.partial(_gateup_body, tn=tn),
        grid=(m // tm, f // tn),
        in_specs=[_resident_rows(tm, d), _side_spec(xsb), wspec, wspec],
        out_specs=[pl.BlockSpec((tm, tn), lambda i, j: (i, j)), _side_spec(side_out)],
        out_shape=[jax.ShapeDtypeStruct((m, f), BF16), side_out],
        compiler_params=_params(("arbitrary", "arbitrary")),
        name="ffn_hidden",
    )(xb, xsb, wg, wu)


def _plain_body(x_ref, xs_ref, w_ref, o_ref, os_ref, *, tn):
    w = w_ref[...]
    o_ref[...] = _dot(x_ref[...], w).astype(o_ref.dtype)

    @pl.when(pl.program_id(0) == 0)
    def _():
        os_ref[:, _col(pl.program_id(1), tn)] = _dot(xs_ref[...], w).astype(os_ref.dtype)


def _matmul(xb, xsb, w, prefix, tm_pref=2048, tn=512, out_dtype=F32):
    m, d = xb.shape
    n = w.shape[-1]
    tm = _row_tile(m, tm_pref)
    side_out = jax.ShapeDtypeStruct((xsb.shape[0], n), out_dtype)
    return pl.pallas_call(
        functools.partial(_plain_body, tn=tn),
        grid=(m // tm, n // tn),
        in_specs=[_resident_rows(tm, d), _side_spec(xsb),
                  _wspec(prefix, d, tn, lambda i, j: (0, j))],
        out_specs=[pl.BlockSpec((tm, tn), lambda i, j: (i, j)), _side_spec(side_out)],
        out_shape=[jax.ShapeDtypeStruct((m, n), out_dtype), side_out],
        compiler_params=_params(("arbitrary", "arbitrary")),
        name="matmul",
    )(xb, xsb, w)


LN_ROWS = 64


def _res_accumulate(of_ref, col, part, x_tile, kh, nkh, coef):
    if nkh == 1:
        of_ref[:, col] = ALPHA * x_tile() + coef * part
        return

    @pl.when(kh == 0)
    def _():
        of_ref[:, col] = part

    @pl.when((kh > 0) & (kh < nkh - 1))
    def _():
        of_ref[:, col] += part

    @pl.when(kh == nkh - 1)
    def _():
        of_ref[:, col] = ALPHA * x_tile() + coef * (of_ref[:, col] + part)


def _layer_norm_rows(of_ref, ob_ref, gain, bias):
    ln_rows = math.gcd(of_ref.shape[0], LN_ROWS)

    def rows(r, carry):
        sl = pl.ds(pl.multiple_of(r * ln_rows, ln_rows), ln_rows)
        mu = jnp.mean(of_ref[sl, :], axis=-1, keepdims=True)
        c = of_ref[sl, :] - mu
        rstd = lax.rsqrt(jnp.mean(c * c, axis=-1, keepdims=True) + LN_EPS)
        y = (of_ref[sl, :] - mu) * rstd * gain + bias
        of_ref[sl, :] = y
        ob_ref[sl, :] = y.astype(ob_ref.dtype)
        return carry

    lax.fori_loop(0, of_ref.shape[0] // ln_rows, rows, 0)


def _res_ln_body(a_ref, as_ref, w_ref, x_ref, xs_ref, g_ref, b_ref, of_ref, ob_ref, ofs_ref, obs_ref,
                 *, coef, nkh, nj, tn, tk):
    i = pl.program_id(0)
    kh = pl.program_id(1)
    j = pl.program_id(2)
    col = _col(j, tn)
    w = w_ref[...]
    last = (kh == nkh - 1) & (j == nj - 1)
    _res_accumulate(of_ref, col, _dot(a_ref[...], w), lambda: x_ref[...], kh, nkh, coef)

    @pl.when(last)
    def _():
        _layer_norm_rows(of_ref, ob_ref, g_ref[...], b_ref[...])

    @pl.when(i == 0)
    def _():
        part = _dot(as_ref[:, pl.ds(pl.multiple_of(kh * tk, LANES), tk)], w)
        _res_accumulate(ofs_ref, col, part, lambda: xs_ref[:, col], kh, nkh, coef)

    @pl.when((i == 0) & last)
    def _():
        _layer_norm_rows(ofs_ref, obs_ref, g_ref[...], b_ref[...])


def _matmul_res_ln(a, a_s, w, prefix, x, x_s, gain, bias, coef, nkh, tm_pref=512, tn=512):
    m, kdim = a.shape
    d = w.shape[-1]
    tm = _row_tile(m, tm_pref)
    tk = kdim // nkh
    nj = d // tn
    ms = a_s.shape[0]
    side_f = jax.ShapeDtypeStruct((ms, d), F32)
    side_b = jax.ShapeDtypeStruct((ms, d), BF16)
    vec = pl.BlockSpec((1, d), lambda i, k, j: (0, 0))
    rows = pl.BlockSpec((tm, d), lambda i, k, j: (i, 0))
    return pl.pallas_call(
        functools.partial(_res_ln_body, coef=coef, nkh=nkh, nj=nj, tn=tn, tk=tk),
        grid=(m // tm, nkh, nj),
        in_specs=[pl.BlockSpec((tm, tk), lambda i, k, j: (i, k)), _side_spec(a_s),
                  _wspec(prefix, tk, tn, lambda i, k, j: (k, j)),
                  pl.BlockSpec((tm, tn), lambda i, k, j: (i, jnp.where(k == nkh - 1, j, 0))),
                  _side_spec(x_s), vec, vec],
        out_specs=[rows, rows, _side_spec(side_f), _side_spec(side_b)],
        out_shape=[jax.ShapeDtypeStruct((m, d), F32), jax.ShapeDtypeStruct((m, d), BF16), side_f, side_b],
        compiler_params=_params(("arbitrary", "arbitrary", "arbitrary")),
        name="matmul_res_ln",
    )(a, a_s, w, x, x_s, gain.reshape(1, d), bias.reshape(1, d))


MERGE_ROWS = 256


def _merge_body(yp_ref, ya_ref, yg_ref, sp_ref, sa_ref, sg_ref, wp_ref, wa_ref, wg_ref,
                gp_ref, ga_ref, gg_ref, hs_ref, o_ref, os_ref, *, tn, g0, nd):
    wp = wp_ref[...]
    wa = wa_ref[...]
    wg = wg_ref[...]
    tm = o_ref.shape[0]
    step = math.gcd(tm, MERGE_ROWS)
    for r0 in range(0, tm, step):
        rows = pl.ds(r0, step)
        acc = jax.nn.sigmoid(gp_ref[rows, :]) * _dot(yp_ref[rows, :], wp)
        acc += jax.nn.sigmoid(ga_ref[rows, :]) * _dot(ya_ref[rows, :], wa)
        acc += jax.nn.sigmoid(gg_ref[rows, :]) * _dot(yg_ref[rows, :], wg)
        o_ref[rows, :] = acc.astype(o_ref.dtype)

    @pl.when(pl.program_id(0) == 0)
    def _():
        j = pl.program_id(1)
        acc_s = None
        for n, (y_ref, w) in enumerate(((sp_ref, wp), (sa_ref, wa), (sg_ref, wg))):
            term = jax.nn.sigmoid(hs_ref[:, _col(g0 + n * nd + j, tn)]) * _dot(y_ref[...], w)
            acc_s = term if acc_s is None else acc_s + term
        os_ref[:, _col(j, tn)] = acc_s.astype(os_ref.dtype)


def _merge(ys, ys_side, ws, prefix, h, h_side, gate_col, tm_pref=1024, tn=512):
    m = ys[0].shape[0]
    d = ws[0].shape[-1]
    tm = _row_tile(m, tm_pref)
    nd = d // tn
    g0 = gate_col // tn
    side_out = jax.ShapeDtypeStruct((h_side.shape[0], d), BF16)

    def yspec(y):
        return pl.BlockSpec((tm, y.shape[1]), lambda i, j: (i, 0))

    def wspec(w):
        return _wspec(prefix, w.shape[-2], tn, lambda i, j: (0, j))

    def gspec(n):
        return pl.BlockSpec((tm, tn), lambda i, j: (i, g0 + n * nd + j))

    return pl.pallas_call(
        functools.partial(_merge_body, tn=tn, g0=g0, nd=nd),
        grid=(m // tm, nd),
        in_specs=([yspec(y) for y in ys] + [_side_spec(y) for y in ys_side] + [wspec(w) for w in ws]
                  + [gspec(0), gspec(1), gspec(2), _side_spec(h_side)]),
        out_specs=[pl.BlockSpec((tm, tn), lambda i, j: (i, j)), _side_spec(side_out)],
        out_shape=[jax.ShapeDtypeStruct((m, d), BF16), side_out],
        compiler_params=_params(("arbitrary", "arbitrary")),
        name="merge",
    )(*ys, *ys_side, *ws, h, h, h, h_side)


def _pool_body(hist_ref, u_ref, w_ref, s_ref, o_ref, ext, *, p0, rows):
    t = o_ref.shape[1]
    gi = pl.program_id(1)
    ext[pl.ds(0, POOL_HALO), :] = hist_ref[0]
    ext[pl.ds(POOL_HALO, t), :] = u_ref[0]
    for idx, win in enumerate(POOL_WINDOWS):
        @pl.when(gi == idx)
        def _(win=win):
            wmat = w_ref[0]
            scale = s_ref[0]
            for r0 in range(0, t, rows):
                cur = ext[pl.ds(POOL_HALO + r0, rows), :]
                tot = cur
                for s in range(1, win):
                    tot = tot + ext[pl.ds(POOL_HALO + r0 - s, rows), :]
                pos = p0 + r0 + lax.broadcasted_iota(jnp.int32, (rows, 1), 0)
                cnt = jnp.minimum(pos + 1, win).astype(F32)
                dlt = tot / cnt - cur
                y = _dot(dlt.astype(BF16), wmat) * scale
                o_ref[0, pl.ds(r0, rows), :] = y.astype(o_ref.dtype)


def _pool(h3, col_u, hist, w_grp, scale, p0):
    b, t, _ = h3.shape
    c = hist.shape[-1]
    g = len(POOL_WINDOWS)
    gc = c // g
    rows = min(t, 256)
    return pl.pallas_call(
        functools.partial(_pool_body, p0=p0, rows=rows),
        grid=(b, g),
        in_specs=[pl.BlockSpec((1, POOL_HALO, gc), lambda i, j: (i, 0, j)),
                  pl.BlockSpec((1, t, gc), lambda i, j: (i, 0, col_u // gc + j)),
                  pl.BlockSpec((1, gc, gc), lambda i, j: (j, 0, 0)),
                  pl.BlockSpec((1, 1, gc), lambda i, j: (j, 0, 0))],
        out_specs=pl.BlockSpec((1, t, gc), lambda i, j: (i, 0, j)),
        out_shape=jax.ShapeDtypeStruct((b, t, c), BF16),
        scratch_shapes=[pltpu.VMEM((POOL_HALO + t, gc), F32)],
        compiler_params=_params(("parallel", "arbitrary")),
        name="pool",
    )(hist, h3, w_grp, scale.reshape(g, 1, gc))


def _t5_bias(rel_bias, g):
    w, d = DILATED_GROUPS[g]
    dist = d * np.arange(w // d + 1)
    max_exact = NUM_BUCKETS // 2
    df = jnp.maximum(dist, 1).astype(F32)
    large = max_exact + (jnp.log(df / max_exact) / math.log(MAX_DISTANCE / max_exact)
                         * (NUM_BUCKETS - max_exact)).astype(jnp.int32)
    bucket = jnp.where(dist < max_exact, dist, jnp.minimum(large, NUM_BUCKETS - 1))
    return rel_bias[bucket][:, g * HEADS:(g + 1) * HEADS].T.astype(F32)


def _neg(rows, n):
    return jnp.full((rows, n), NEG_INF, F32)


def _toeplitz(p):
    h, n = p.shape
    b = n // 2
    return jnp.tile(p, (1, b))[:, :b * (n - 1)].reshape(h, b, n - 1)[:, :, :b]


def _band_tables(rel_bias):
    blk = ATTN_BLOCK
    tabs = []
    for g in range(N_GROUPS):
        bias = _t5_bias(rel_bias, g)
        cur = _toeplitz(jnp.concatenate([bias[:, :1], _neg(HEADS, blk), jnp.flip(bias[:, 1:blk], 1)], 1))
        prev = _toeplitz(jnp.concatenate([jnp.flip(bias[:, 1:blk + 1], 1), _neg(HEADS, blk)], 1))
        tabs.append(jnp.stack([cur, prev], axis=1))
    return jnp.stack(tabs)


def _softmax_block(scores, values):
    same = all(s.shape == scores[0].shape for s in scores)
    if same:
        m = functools.reduce(jnp.maximum, scores).max(axis=-1, keepdims=True)
    else:
        m = functools.reduce(jnp.maximum, [s.max(axis=-1, keepdims=True) for s in scores])
    ps = [jnp.exp(s - m) for s in scores]
    if same:
        l = functools.reduce(jnp.add, ps).sum(axis=-1, keepdims=True)
    else:
        l = functools.reduce(jnp.add, [p.sum(axis=-1, keepdims=True) for p in ps])
    o = functools.reduce(jnp.add, [_dot(p.astype(BF16), v.astype(BF16)) for p, v in zip(ps, values)])
    return m, l, o


def _attn_prompt_body(q0, k0, v0, q1, k1, v1, q2, k2, v2, tab_ref, o_ref, kv0, kv1, kv2,
                      oacc, macc, lacc, *, t):
    qkv = ((q0, k0, v0), (q1, k1, v1), (q2, k2, v2))
    scale = HEAD_DIM ** -0.5
    head = pl.program_id(1)
    per_pos = 2 * HEADS
    for (w, _), kv_ref, (_, k_ref, v_ref) in zip(DILATED_GROUPS, (kv0, kv1, kv2), qkv):
        n = min(w, t)
        kv_ref[0, pl.ds(head, n, stride=per_pos), :] = k_ref[0, pl.ds(t - n, n), :]
        kv_ref[0, pl.ds(HEADS + head, n, stride=per_pos), :] = v_ref[0, pl.ds(t - n, n), :]
    for g, (_, d) in enumerate(DILATED_GROUPS):
        q_ref, k_ref, v_ref = qkv[g]
        span = ATTN_BLOCK * d

        def rows(start, ref):
            if d == 1:
                return ref[0, pl.ds(start, ATTN_BLOCK), :]
            return ref[0, pl.ds(start, ATTN_BLOCK, stride=d), :]

        for s in range(t // span):
            for r in range(d):
                start = s * span + r
                qb = rows(start, q_ref).astype(BF16)
                scores = [_dot_nt(qb, rows(start, k_ref).astype(BF16)) * scale + tab_ref[g, 0, 0]]
                values = [rows(start, v_ref)]
                if s > 0:
                    scores.append(_dot_nt(qb, rows(start - span, k_ref).astype(BF16)) * scale + tab_ref[g, 0, 1])
                    values.append(rows(start - span, v_ref))
                m, l, o = _softmax_block(scores, values)
                if d == 1:
                    sl = pl.ds(start, ATTN_BLOCK)
                else:
                    sl = pl.ds(start, ATTN_BLOCK, stride=d)
                oacc[g, sl, :] = o
                macc[g, sl, :] = jnp.broadcast_to(m, (ATTN_BLOCK, LANES))
                lacc[g, sl, :] = jnp.broadcast_to(l, (ATTN_BLOCK, LANES))
    for r0 in range(0, t, ATTN_BLOCK):
        sl = pl.ds(r0, ATTN_BLOCK)
        ms = [macc[g, sl, :] for g in range(N_GROUPS)]
        mm = jnp.maximum(jnp.maximum(ms[0], ms[1]), ms[2])
        den = None
        num = None
        for g in range(N_GROUPS):
            c = jnp.exp(ms[g] - mm)
            dn = lacc[g, sl, :] * c
            nm = oacc[g, sl, :] * c
            den = dn if den is None else den + dn
            num = nm if num is None else num + nm
        o_ref[0, sl, :] = (num / den).astype(o_ref.dtype)


def _attn_prompt(h3, tabs, col_q, col_k, col_v):
    b, t, _ = h3.shape
    assert t % (ATTN_BLOCK * max(d for _, d in DILATED_GROUPS)) == 0
    per_pos = 2 * HEADS
    wins = [min(w, t) for w, _ in DILATED_GROUPS]

    def spec(col, g):
        blk = col // HEAD_DIM + HEADS * g
        return pl.BlockSpec((1, t, HEAD_DIM), lambda i, h, blk=blk: (i, 0, blk + h))

    in_specs = []
    for g in range(N_GROUPS):
        in_specs += [spec(col_q, g), spec(col_k, g), spec(col_v, g)]
    in_specs.append(pl.BlockSpec((N_GROUPS, 1, 2, ATTN_BLOCK, ATTN_BLOCK), lambda i, h: (0, h, 0, 0, 0)))
    outs = pl.pallas_call(
        functools.partial(_attn_prompt_body, t=t),
        grid=(b, HEADS),
        in_specs=in_specs,
        out_specs=[pl.BlockSpec((1, t, HEAD_DIM), lambda i, h: (i, 0, h))]
                  + [pl.BlockSpec((1, n * per_pos, HEAD_DIM), lambda i, h: (i, 0, 0)) for n in wins],
        out_shape=[jax.ShapeDtypeStruct((b, t, HEADS * HEAD_DIM), BF16)]
                  + [jax.ShapeDtypeStruct((b, n * per_pos, HEAD_DIM), F32) for n in wins],
        scratch_shapes=[pltpu.VMEM((N_GROUPS, t, HEAD_DIM), F32),
                        pltpu.VMEM((N_GROUPS, t, LANES), F32),
                        pltpu.VMEM((N_GROUPS, t, LANES), F32)],
        compiler_params=_params(("arbitrary", "arbitrary")),
        name="attn_prompt",
    )(*([h3] * 9), tabs)
    return outs[0], [kv.reshape(b, n, 2, HEADS, HEAD_DIM) for kv, n in zip(outs[1:], wins)]


def _decode_tables(rel_bias, t_new, t_pad):
    tabs_c, tabs_n = [], []
    for g, (w, d) in enumerate(DILATED_GROUPS):
        bias = _t5_bias(rel_bias, g)
        by_dist = jnp.pad(bias[:, :, None], ((0, 0), (0, 0), (0, d - 1)), constant_values=NEG_INF)
        by_dist = by_dist.reshape(HEADS, -1)[:, :w + 1]
        rows_c, rows_n = [], []
        for t in range(t_pad):
            if t < t_new:
                rows_c.append(jnp.flip(jnp.concatenate([by_dist[:, t + 1:], _neg(HEADS, t)], 1), 1))
                rows_n.append(jnp.concatenate([jnp.flip(by_dist[:, :t + 1], 1), _neg(HEADS, t_pad - t - 1)], 1))
            else:
                rows_c.append(_neg(HEADS, w))
                rows_n.append(_neg(HEADS, t_pad))
        tabs_c.append(jnp.stack(rows_c, axis=1))
        tabs_n.append(jnp.stack(rows_n, axis=1))
    return tabs_c, tabs_n


def _attn_decode_body(q_ref, kn_ref, vn_ref, c0, c1, c2, tc0, tc1, tc2, tn0, tn1, tn2, o_ref):
    caches = (c0, c1, c2)
    tcs = (tc0, tc1, tc2)
    tns = (tn0, tn1, tn2)
    scale = HEAD_DIM ** -0.5
    per_pos = 2 * HEADS
    for h in range(HEADS):
        stats = []
        for g in range(N_GROUPS):
            col = pl.ds((g * HEADS + h) * HEAD_DIM, HEAD_DIM)
            qb = q_ref[0, :, col].astype(BF16)
            w = caches[g].shape[1] // per_pos
            kc = caches[g][0, pl.ds(h, w, stride=per_pos), :]
            vc = caches[g][0, pl.ds(HEADS + h, w, stride=per_pos), :]
            scores = [_dot_nt(qb, kc.astype(BF16)) * scale + tcs[g][h],
                      _dot_nt(qb, kn_ref[0, :, col].astype(BF16)) * scale + tns[g][h]]
            stats.append(_softmax_block(scores, [vc, vn_ref[0, :, col]]))
        mm = jnp.maximum(jnp.maximum(stats[0][0], stats[1][0]), stats[2][0])
        den = None
        num = None
        for m, l, o in stats:
            c = jnp.exp(m - mm)
            den = l * c if den is None else den + l * c
            num = o * c if num is None else num + o * c
        o_ref[0, :, pl.ds(h * HEAD_DIM, HEAD_DIM)] = (num / den).astype(o_ref.dtype)


def _attn_decode(q, kn, vn, caches, l, tabs_c, tabs_n):
    b, tp, _ = q.shape

    def full(a):
        nd = a.ndim
        return pl.BlockSpec(a.shape, lambda i, nd=nd: (0,) * nd)

    def per_batch(a):
        return pl.BlockSpec((1,) + a.shape[1:], lambda i: (i, 0, 0))

    def cache(a):
        return pl.BlockSpec((None, 1) + a.shape[2:], lambda i: (l, i, 0, 0))

    args = [q, kn, vn, *caches, *tabs_c, *tabs_n]
    in_specs = ([per_batch(a) for a in args[:3]] + [cache(a) for a in args[3:6]]
                + [full(a) for a in args[6:]])
    return pl.pallas_call(
        _attn_decode_body,
        grid=(b,),
        in_specs=in_specs,
        out_specs=pl.BlockSpec((1, tp, HEADS * HEAD_DIM), lambda i: (i, 0, 0)),
        out_shape=jax.ShapeDtypeStruct((b, tp, HEADS * HEAD_DIM), BF16),
        compiler_params=_params(("parallel",)),
        name="attn_decode",
    )(*args)


def _gla_body(q_ref, k_ref, v_ref, r_ref, low_ref, wg_ref, bg_ref, nrm_ref, s0_ref, y_ref, s_ref, st,
              *, chunk, valid, hk, hv):
    ti = pl.program_id(1)
    tt = q_ref.shape[1]

    @pl.when(ti == 0)
    def _():
        st[...] = s0_ref[0]

    wgate = wg_ref[...]
    bgate = bg_ref[...]
    nrm = nrm_ref[...]
    row = lax.broadcasted_iota(jnp.int32, (chunk, chunk), 0)
    colm = lax.broadcasted_iota(jnp.int32, (chunk, chunk), 1)
    causal = row >= colm
    tri = causal.astype(F32)
    live = lax.broadcasted_iota(jnp.int32, (chunk, 1), 0) < valid

    def step(c, carry):
        sl = pl.ds(pl.multiple_of(c * chunk, chunk), chunk)
        z = _dot(low_ref[0, sl, :].astype(BF16), wgate) + bgate
        glog = jnp.where(live, jax.nn.log_sigmoid(z) / GLA_TAU, 0.0)
        gcum = jnp.dot(tri, glog, precision=lax.Precision.HIGHEST, preferred_element_type=F32)
        glast = gcum[chunk - 1:chunk, :]
        kk = k_ref[0, sl, :]
        qg = ((q_ref[0, sl, :] * (hk ** -0.5)) * jnp.exp(gcum)).astype(BF16)
        kdn = (kk * jnp.exp(-gcum)).astype(BF16)
        kdec = (kk * jnp.exp(glast - gcum)).astype(BF16)
        decay = jnp.exp(glast)
        for h in range(HEADS):
            ks = slice(h * GLA_HK_PAD, (h + 1) * GLA_HK_PAD)
            vs = pl.ds(h * hv, hv)
            vv = v_ref[0, sl, vs].astype(BF16)
            a = jnp.where(causal, _dot_nt(qg[:, ks], kdn[:, ks]), 0.0)
            s_old = st[h]
            o = _dot_nt(qg[:, ks], s_old.astype(BF16)) + _dot(a.astype(BF16), vv)
            st[h] = decay[:, ks] * s_old + _dot_tn(vv, kdec[:, ks])
            o = o * lax.rsqrt(jnp.mean(o * o, axis=-1, keepdims=True) + RMS_EPS) * nrm
            y_ref[0, sl, vs] = (o * jax.nn.silu(r_ref[0, sl, vs])).astype(y_ref.dtype)
        return carry

    lax.fori_loop(0, tt // chunk, step, 0, unroll=min(2, tt // chunk))

    @pl.when(ti == pl.num_programs(1) - 1)
    def _():
        s_ref[0] = st[...]


GLA_TIME_TILE = 512


def _gla(h3, wgate, bgate, nrm, s0t, cols, chunk, valid, hk):
    b, t, _ = h3.shape
    hv = nrm.shape[-1]
    col_q, col_k, col_v, col_r, col_low = cols
    tt = _row_tile(t, GLA_TIME_TILE)
    kw = HEADS * GLA_HK_PAD
    vw = HEADS * hv

    def spec(col, width):
        return pl.BlockSpec((1, tt, width), lambda i, j: (i, j, col // width))

    state = pl.BlockSpec((1, HEADS, hv, GLA_HK_PAD), lambda i, j: (i, 0, 0, 0))
    return pl.pallas_call(
        functools.partial(_gla_body, chunk=chunk, valid=valid, hk=hk, hv=hv),
        grid=(b, t // tt),
        in_specs=[spec(col_q, kw), spec(col_k, kw), spec(col_v, vw), spec(col_r, vw), spec(col_low, LANES),
                  pl.BlockSpec((LANES, kw), lambda i, j: (0, 0)),
                  pl.BlockSpec((1, kw), lambda i, j: (0, 0)),
                  pl.BlockSpec((1, hv), lambda i, j: (0, 0)),
                  state],
        out_specs=[pl.BlockSpec((1, tt, vw), lambda i, j: (i, j, 0)), state],
        out_shape=[jax.ShapeDtypeStruct((b, t, vw), BF16),
                   jax.ShapeDtypeStruct((b, HEADS, hv, GLA_HK_PAD), F32)],
        scratch_shapes=[pltpu.VMEM((HEADS, hv, GLA_HK_PAD), F32)],
        compiler_params=_params(("parallel", "arbitrary")),
        name="gla",
    )(h3, h3, h3, h3, h3, wgate, bgate, nrm.reshape(1, hv), s0t)


def _pad_last(a, width):
    return jnp.pad(a, ((0, 0),) * (a.ndim - 1) + ((0, width - a.shape[-1]),))


def _pad_heads(a, hk):
    lead = a.shape[:-1]
    return _pad_last(a.reshape(lead + (HEADS, hk)), GLA_HK_PAD).reshape(lead + (HEADS * GLA_HK_PAD,))


def _layout(d_model):
    pool_w = 3 * d_model // 8
    attn_w = N_GROUPS * HEADS * HEAD_DIM
    dv = 3 * d_model // 8
    dk = dv // 2
    sections = (pool_w, attn_w, attn_w, attn_w, dk, dk, dv, GLA_RANK, dv, 3 * d_model)
    src = {}
    acc = 0
    for name, w in zip(("u", "aq", "ak", "av", "gq", "gk", "gv", "glow", "gr", "gates"), sections):
        src[name] = (acc, w)
        acc += w
    order = ("u", "aq", "ak", "av", "gv", "gr", "gq", "gk", "glow")
    widths = {"u": pool_w, "aq": attn_w, "ak": attn_w, "av": attn_w, "gv": dv, "gr": dv,
              "gq": HEADS * GLA_HK_PAD, "gk": HEADS * GLA_HK_PAD, "glow": 512}
    dst = {}
    acc = 0
    for name in order:
        dst[name] = acc
        acc += widths[name]
    return src, dst, widths, order, dk // HEADS, dv // HEADS


def _prep_weights(d_model, ffn_w_gate, ffn_w_up, ffn_w_down, w_in, pool_w, gla_w_gate, gla_b_gate,
                  w_br_pool, w_br_attn, w_br_gla, w_out):
    src, dst, widths, order, hk, hv = _layout(d_model)
    parts = []
    for name in order + ("gates",):
        off, w = src[name]
        blk = w_in[:, :, off:off + w]
        if name in ("gq", "gk"):
            blk = _pad_heads(blk, hk)
        elif name == "glow":
            blk = _pad_last(blk, widths[name])
        parts.append(blk)
    wgate = _pad_heads(jnp.pad(gla_w_gate, ((0, 0), (0, LANES - GLA_RANK), (0, 0))), hk)
    bgate = _pad_heads(gla_b_gate, hk)[:, None, :]
    return dict(
        wg=ffn_w_gate, wu=ffn_w_up,
        wd=ffn_w_down.astype(BF16),
        w_in=jnp.concatenate(parts, axis=2).astype(BF16),
        gate_col=sum(widths[n] for n in order),
        pool_w=pool_w.astype(BF16),
        gla_wgate=wgate.astype(BF16),
        gla_bgate=bgate.astype(F32),
        w_br_pool=w_br_pool.astype(BF16),
        w_br_attn=w_br_attn.astype(BF16),
        w_br_gla=w_br_gla.astype(BF16),
        w_out=w_out.astype(BF16),
    )


def _ffn_block(xp, xs, pw, l, i, gain, bias):
    hid_p, hid_s = _ffn_hidden(xp[1], xs[1], pw["wg"], pw["wu"], (l, i))
    of, ob, ofs, obs = _matmul_res_ln(hid_p, hid_s, pw["wd"], (l, i), xp[0], xs[0], gain, bias, 0.5, nkh=2)
    return (of, ob), (ofs, obs)


def _kv_rows(h3, dst, g, rows):
    gw = HEADS * HEAD_DIM
    b = h3.shape[0]
    k = h3[:, rows, dst["ak"] + g * gw: dst["ak"] + (g + 1) * gw]
    v = h3[:, rows, dst["av"] + g * gw: dst["av"] + (g + 1) * gw]
    return jnp.stack([k, v], axis=2).reshape(b, k.shape[1], 2, HEADS, HEAD_DIM)


def _mixers(h, l, bsz, t_new, decode, pw, lw, caches):
    m = h.shape[0]
    src, dst, widths, order, hk, hv = _layout(lw["ln_gain"].shape[-1])
    h3 = h.reshape(bsz, t_new, -1)
    pool_c = widths["u"]
    u = h3[:, :, dst["u"]:dst["u"] + pool_c]
    if decode:
        pool_buf, kv_bufs, gla_s = caches
        t_pad = SUBLANES
        p0 = PAST_LEN
        h3p = jnp.pad(h3, ((0, 0), (0, t_pad - t_new), (0, 0)))
        hist = pool_buf
    else:
        t_pad = t_new
        p0 = 0
        h3p = h3
        hist = jnp.zeros((bsz, POOL_KEEP, pool_c), F32)
        gla_s = jnp.zeros((bsz, HEADS, hk, hv), F32)
    hist_ext = jnp.pad(hist, ((0, 0), (POOL_HALO - POOL_KEEP, 0), (0, 0)))
    y_pool = _pool(h3p, dst["u"], hist_ext, pw["pool_w"][l], lw["pool_scale"], p0)[:, :t_new]
    new_pool = jnp.concatenate([hist, u], axis=1)[:, -POOL_KEEP:]
    aw = widths["aq"]
    if decode:
        q, kn, vn = (h3p[:, :, dst[n]:dst[n] + aw] for n in ("aq", "ak", "av"))
        flat = [c.reshape(c.shape[0], bsz, c.shape[2] * 2 * HEADS, HEAD_DIM) for c in kv_bufs]
        y_attn = _attn_decode(q, kn, vn, flat, l, *lw["decode_tabs"])[:, :t_new]
        new_kv = [_kv_rows(h3, dst, g, slice(None)) for g in range(N_GROUPS)]
    else:
        y_attn, new_kv = _attn_prompt(h3, lw["band_tabs"], dst["aq"], dst["ak"], dst["av"])
    s0t = jnp.pad(jnp.swapaxes(gla_s.astype(F32), -1, -2), ((0, 0), (0, 0), (0, 0), (0, GLA_HK_PAD - hk)))
    chunk = SUBLANES if decode else math.gcd(t_new, GLA_CHUNK)
    y_gla, st = _gla(h3p, pw["gla_wgate"][l], pw["gla_bgate"][l], lw["gla_norm"], s0t,
                     (dst["gq"], dst["gk"], dst["gv"], dst["gr"], dst["glow"]), chunk,
                     t_new if decode else chunk, hk)
    y_gla = y_gla[:, :t_new]
    new_s = jnp.swapaxes(st[..., :hk], -1, -2)
    ys = [y.reshape(m, -1) for y in (y_pool, y_attn, y_gla)]
    return ys, new_pool, new_kv, new_s


def _layer(xp, xs, l, shape_p, shape_s, pw, lw, caches):
    xp, xs = _ffn_block(xp, xs, pw, l, 0, lw["ln_gain"][0], lw["ln_bias"][0])
    h_p, h_s = _matmul(xp[1], xs[1], pw["w_in"], (l,))
    ys_p, *new_p = _mixers(h_p, l, *shape_p, False, pw, lw, None)
    ys_s, *new_s = _mixers(h_s, l, *shape_s, True, pw, lw, caches)
    merged_p, merged_s = _merge(ys_p, ys_s, [pw["w_br_pool"], pw["w_br_attn"], pw["w_br_gla"]], (l,),
                                h_p, h_s, pw["gate_col"])
    of, ob, ofs, obs = _matmul_res_ln(merged_p, merged_s, pw["w_out"], (l,), xp[0], xs[0],
                                      lw["ln_gain"][1], lw["ln_bias"][1], 1.0, nkh=1)
    xp, xs = _ffn_block((of, ob), (ofs, obs), pw, l, 1, lw["ln_gain"][2], lw["ln_bias"][2])
    return xp, xs, new_p, new_s


def kernel(x_prompt, x_sample, cache_pool, cache_kv_w128, cache_kv_w512, cache_kv_w2048, state_gla, rel_bias, ln_gain, ln_bias, ffn_w_gate, ffn_w_up, ffn_w_down, w_in, pool_w, pool_scale, gla_w_gate, gla_b_gate, gla_norm, w_br_pool, w_br_attn, w_br_gla, w_out):
    d_model = x_prompt.shape[-1]
    pw = _prep_weights(d_model, ffn_w_gate, ffn_w_up, ffn_w_down, w_in, pool_w, gla_w_gate,
                       gla_b_gate, w_br_pool, w_br_attn, w_br_gla, w_out)
    band_tabs = _band_tables(rel_bias)
    decode_tabs = _decode_tables(rel_bias, x_sample.shape[1], SUBLANES)
    kv_caches = [cache_kv_w128, cache_kv_w512, cache_kv_w2048]
    shape_p = x_prompt.shape[:2]
    shape_s = x_sample.shape[:2]
    xp = x_prompt.reshape(-1, d_model)
    xs = x_sample.reshape(-1, d_model)
    xp = (xp, xp.astype(BF16))
    xs = (xs, xs.astype(BF16))
    outs_p, outs_s = [], []
    for l in range(DEPTH):
        lw = dict(ln_gain=ln_gain[l], ln_bias=ln_bias[l], pool_scale=pool_scale[l], gla_norm=gla_norm[l],
                  band_tabs=band_tabs, decode_tabs=decode_tabs)
        xp, xs, new_p, new_s = _layer(xp, xs, l, shape_p, shape_s, pw, lw,
                                      (cache_pool[l], kv_caches, state_gla[l]))
        outs_p.append(new_p)
        outs_s.append(new_s)

    def stacked(outs):
        pools = jnp.stack([o[0] for o in outs])
        kvs = [jnp.stack([o[1][g] for o in outs]) for g in range(N_GROUPS)]
        return pools, kvs, jnp.stack([o[2] for o in outs])

    pool_p, kv_p, gla_p = stacked(outs_p)
    pool_s, rows_s, gla_s = stacked(outs_s)
    t_s = shape_s[1]
    kv_s = [jnp.concatenate([c[:, :, t_s:], r], axis=2) for c, r in zip(kv_caches, rows_s)]
    return (xp[0].reshape(x_prompt.shape), xs[0].reshape(x_sample.shape),
            pool_p, kv_p[0], kv_p[1], kv_p[2], gla_p,
            pool_s, kv_s[0], kv_s[1], kv_s[2], gla_s)
```

```python
import functools
import math

import numpy as np
import jax
import jax.numpy as jnp
from jax import lax
from jax.experimental import pallas as pl
from jax.experimental.pallas import tpu as pltpu

DEPTH = 2
PAST_LEN = 16384
POOL_WINDOWS = (2, 4, 8, 16)
POOL_KEEP = max(POOL_WINDOWS) - 1
DILATED_GROUPS = ((128, 1), (512, 4), (2048, 16))
N_GROUPS = len(DILATED_GROUPS)
HEADS = 4
HEAD_DIM = 128
NUM_BUCKETS = 32
MAX_DISTANCE = 2048
GLA_RANK = 16
GLA_TAU = 16.0
GLA_CHUNK = 64
ALPHA = (2 * DEPTH) ** 0.25
LN_EPS = 1e-5
RMS_EPS = 1e-6
NEG_INF = -1e30

LANES = 128
SUBLANES = 8
MXU_DIM = 256
VMEM_LIMIT = 60 * 1024 * 1024

BF16 = jnp.bfloat16
F32 = jnp.float32

GLA_HK_PAD = 256
ATTN_BLOCK = 128
POOL_HALO = 16


def _params(sem):
    return pltpu.CompilerParams(dimension_semantics=sem, vmem_limit_bytes=VMEM_LIMIT)


def _dot(a, b):
    return jnp.dot(a, b, preferred_element_type=F32)


def _dot_nt(a, b):
    return lax.dot_general(a, b, (((1,), (1,)), ((), ())), preferred_element_type=F32)


def _dot_tn(a, b):
    return lax.dot_general(a, b, (((0,), (0,)), ((), ())), preferred_element_type=F32)


def _row_tile(m, pref):
    return pref if m % pref == 0 else m


def _wspec(prefix, rows, cols, index):
    return pl.BlockSpec((None,) * len(prefix) + (rows, cols), lambda *g: tuple(prefix) + tuple(index(*g)))


def _side_spec(a):
    nd = len(a.shape)
    return pl.BlockSpec(tuple(a.shape), lambda *g: (0,) * nd)


def _col(j, tn):
    return pl.ds(pl.multiple_of(j * tn, tn), tn)


def _gateup_body(x_ref, xs_ref, wg_ref, wu_ref, o_ref, os_ref, *, tn):
    wg = wg_ref[...].astype(BF16)
    wu = wu_ref[...].astype(BF16)
    x = x_ref[...]
    o_ref[...] = (jax.nn.silu(_dot(x, wg)) * _dot(x, wu)).astype(o_ref.dtype)

    @pl.when(pl.program_id(0) == 0)
    def _():
        xs = xs_ref[...]
        os_ref[:, _col(pl.program_id(1), tn)] = (jax.nn.silu(_dot(xs, wg)) * _dot(xs, wu)).astype(os_ref.dtype)


def _resident_rows(tm, d):
    return pl.BlockSpec((tm, d), lambda i, j: (i, 0), pipeline_mode=pl.Buffered(1))


def _ffn_hidden(xb, xsb, wg, wu, prefix, tm_pref=2048, tn=256):
    m, d = xb.shape
    f = wg.shape[-1]
    tm = _row_tile(m, tm_pref)
    wspec = _wspec(prefix, d, tn, lambda i, j: (0, j))
    side_out = jax.ShapeDtypeStruct((xsb.shape[0], f), BF16)
    return pl.pallas_call(
        functools.partial(_gateup_body, tn=tn),
        grid=(m // tm, f // tn),
        in_specs=[_resident_rows(tm, d), _side_spec(xsb), wspec, wspec],
        out_specs=[pl.BlockSpec((tm, tn), lambda i, j: (i, j)), _side_spec(side_out)],
        out_shape=[jax.ShapeDtypeStruct((m, f), BF16), side_out],
        compiler_params=_params(("arbitrary", "arbitrary")),
        name="ffn_hidden",
    )(xb, xsb, wg, wu)


def _plain_body(x_ref, xs_ref, w_ref, o_ref, os_ref, *, tn):
    w = w_ref[...]
    o_ref[...] = _dot(x_ref[...], w).astype(o_ref.dtype)

    @pl.when(pl.program_id(0) == 0)
    def _():
        os_ref[:, _col(pl.program_id(1), tn)] = _dot(xs_ref[...], w).astype(os_ref.dtype)


def _matmul(xb, xsb, w, prefix, tm_pref=2048, tn=512, out_dtype=F32):
    m, d = xb.shape
    n = w.shape[-1]
    tm = _row_tile(m, tm_pref)
    side_out = jax.ShapeDtypeStruct((xsb.shape[0], n), out_dtype)
    return pl.pallas_call(
        functools.partial(_plain_body, tn=tn),
        grid=(m // tm, n // tn),
        in_specs=[_resident_rows(tm, d), _side_spec(xsb),
                  _wspec(prefix, d, tn, lambda i, j: (0, j))],
        out_specs=[pl.BlockSpec((tm, tn), lambda i, j: (i, j)), _side_spec(side_out)],
        out_shape=[jax.ShapeDtypeStruct((m, n), out_dtype), side_out],
        compiler_params=_params(("arbitrary", "arbitrary")),
        name="matmul",
    )(xb, xsb, w)


LN_ROWS = 128


def _res_accumulate(of_ref, col, part, x_tile, kh, nkh, coef):
    if nkh == 1:
        of_ref[:, col] = ALPHA * x_tile() + coef * part
        return

    @pl.when(kh == 0)
    def _():
        of_ref[:, col] = part

    @pl.when((kh > 0) & (kh < nkh - 1))
    def _():
        of_ref[:, col] += part

    @pl.when(kh == nkh - 1)
    def _():
        of_ref[:, col] = ALPHA * x_tile() + coef * (of_ref[:, col] + part)


def _layer_norm_rows(of_ref, ob_ref, gain, bias):
    ln_rows = math.gcd(of_ref.shape[0], LN_ROWS)

    def rows(r, carry):
        sl = pl.ds(pl.multiple_of(r * ln_rows, ln_rows), ln_rows)
        mu = jnp.mean(of_ref[sl, :], axis=-1, keepdims=True)
        c = of_ref[sl, :] - mu
        rstd = lax.rsqrt(jnp.mean(c * c, axis=-1, keepdims=True) + LN_EPS)
        y = (of_ref[sl, :] - mu) * rstd * gain + bias
        of_ref[sl, :] = y
        ob_ref[sl, :] = y.astype(ob_ref.dtype)
        return carry

    lax.fori_loop(0, of_ref.shape[0] // ln_rows, rows, 0)


def _res_ln_body(a_ref, as_ref, w_ref, x_ref, xs_ref, g_ref, b_ref, of_ref, ob_ref, ofs_ref, obs_ref,
                 *, coef, nkh, nj, tn, tk):
    i = pl.program_id(0)
    kh = pl.program_id(1)
    j = pl.program_id(2)
    col = _col(j, tn)
    w = w_ref[...]
    last = (kh == nkh - 1) & (j == nj - 1)
    _res_accumulate(of_ref, col, _dot(a_ref[...], w), lambda: x_ref[...], kh, nkh, coef)

    @pl.when(last)
    def _():
        _layer_norm_rows(of_ref, ob_ref, g_ref[...], b_ref[...])

    @pl.when(i == 0)
    def _():
        part = _dot(as_ref[:, pl.ds(pl.multiple_of(kh * tk, LANES), tk)], w)
        _res_accumulate(ofs_ref, col, part, lambda: xs_ref[:, col], kh, nkh, coef)

    @pl.when((i == 0) & last)
    def _():
        _layer_norm_rows(ofs_ref, obs_ref, g_ref[...], b_ref[...])


def _matmul_res_ln(a, a_s, w, prefix, x, x_s, gain, bias, coef, nkh, tm_pref=512, tn=512):
    m, kdim = a.shape
    d = w.shape[-1]
    tm = _row_tile(m, tm_pref)
    tk = kdim // nkh
    nj = d // tn
    ms = a_s.shape[0]
    side_f = jax.ShapeDtypeStruct((ms, d), F32)
    side_b = jax.ShapeDtypeStruct((ms, d), BF16)
    vec = pl.BlockSpec((1, d), lambda i, k, j: (0, 0))
    rows = pl.BlockSpec((tm, d), lambda i, k, j: (i, 0))
    return pl.pallas_call(
        functools.partial(_res_ln_body, coef=coef, nkh=nkh, nj=nj, tn=tn, tk=tk),
        grid=(m // tm, nkh, nj),
        in_specs=[pl.BlockSpec((tm, tk), lambda i, k, j: (i, k)), _side_spec(a_s),
                  _wspec(prefix, tk, tn, lambda i, k, j: (k, j)),
                  pl.BlockSpec((tm, tn), lambda i, k, j: (i, jnp.where(k == nkh - 1, j, 0))),
                  _side_spec(x_s), vec, vec],
        out_specs=[rows, rows, _side_spec(side_f), _side_spec(side_b)],
        out_shape=[jax.ShapeDtypeStruct((m, d), F32), jax.ShapeDtypeStruct((m, d), BF16), side_f, side_b],
        compiler_params=_params(("arbitrary", "arbitrary", "arbitrary")),
        name="matmul_res_ln",
    )(a, a_s, w, x, x_s, gain.reshape(1, d), bias.reshape(1, d))


MERGE_ROWS = 256


def _merge_body(yp_ref, ya_ref, yg_ref, sp_ref, sa_ref, sg_ref, wp_ref, wa_ref, wg_ref,
                gp_ref, ga_ref, gg_ref, hs_ref, o_ref, os_ref, *, tn, g0, nd):
    wp = wp_ref[...]
    wa = wa_ref[...]
    wg = wg_ref[...]
    tm = o_ref.shape[0]
    step = math.gcd(tm, MERGE_ROWS)
    for r0 in range(0, tm, step):
        rows = pl.ds(r0, step)
        acc = jax.nn.sigmoid(gp_ref[rows, :]) * _dot(yp_ref[rows, :], wp)
        acc += jax.nn.sigmoid(ga_ref[rows, :]) * _dot(ya_ref[rows, :], wa)
        acc += jax.nn.sigmoid(gg_ref[rows, :]) * _dot(yg_ref[rows, :], wg)
        o_ref[rows, :] = acc.astype(o_ref.dtype)

    @pl.when(pl.program_id(0) == 0)
    def _():
        j = pl.program_id(1)
        acc_s = None
        for n, (y_ref, w) in enumerate(((sp_ref, wp), (sa_ref, wa), (sg_ref, wg))):
            term = jax.nn.sigmoid(hs_ref[:, _col(g0 + n * nd + j, tn)]) * _dot(y_ref[...], w)
            acc_s = term if acc_s is None else acc_s + term
        os_ref[:, _col(j, tn)] = acc_s.astype(os_ref.dtype)


def _merge(ys, ys_side, ws, prefix, h, h_side, gate_col, tm_pref=1024, tn=512):
    m = ys[0].shape[0]
    d = ws[0].shape[-1]
    tm = _row_tile(m, tm_pref)
    nd = d // tn
    g0 = gate_col // tn
    side_out = jax.ShapeDtypeStruct((h_side.shape[0], d), BF16)

    def yspec(y):
        return pl.BlockSpec((tm, y.shape[1]), lambda i, j: (i, 0))

    def wspec(w):
        return _wspec(prefix, w.shape[-2], tn, lambda i, j: (0, j))

    def gspec(n):
        return pl.BlockSpec((tm, tn), lambda i, j: (i, g0 + n * nd + j))

    return pl.pallas_call(
        functools.partial(_merge_body, tn=tn, g0=g0, nd=nd),
        grid=(m // tm, nd),
        in_specs=([yspec(y) for y in ys] + [_side_spec(y) for y in ys_side] + [wspec(w) for w in ws]
                  + [gspec(0), gspec(1), gspec(2), _side_spec(h_side)]),
        out_specs=[pl.BlockSpec((tm, tn), lambda i, j: (i, j)), _side_spec(side_out)],
        out_shape=[jax.ShapeDtypeStruct((m, d), BF16), side_out],
        compiler_params=_params(("arbitrary", "arbitrary")),
        name="merge",
    )(*ys, *ys_side, *ws, h, h, h, h_side)


def _pool_body(hist_ref, u_ref, w_ref, s_ref, o_ref, ext, *, p0, rows):
    t = o_ref.shape[1]
    gi = pl.program_id(1)
    ext[pl.ds(0, POOL_HALO), :] = hist_ref[0]
    ext[pl.ds(POOL_HALO, t), :] = u_ref[0]
    for idx, win in enumerate(POOL_WINDOWS):
        @pl.when(gi == idx)
        def _(win=win):
            wmat = w_ref[0]
            scale = s_ref[0]
            for r0 in range(0, t, rows):
                cur = ext[pl.ds(POOL_HALO + r0, rows), :]
                tot = cur
                for s in range(1, win):
                    tot = tot + ext[pl.ds(POOL_HALO + r0 - s, rows), :]
                pos = p0 + r0 + lax.broadcasted_iota(jnp.int32, (rows, 1), 0)
                cnt = jnp.minimum(pos + 1, win).astype(F32)
                dlt = tot / cnt - cur
                y = _dot(dlt.astype(BF16), wmat) * scale
                o_ref[0, pl.ds(r0, rows), :] = y.astype(o_ref.dtype)


def _pool(h3, col_u, hist, w_grp, scale, p0):
    b, t, _ = h3.shape
    c = hist.shape[-1]
    g = len(POOL_WINDOWS)
    gc = c // g
    rows = min(t, 256)
    return pl.pallas_call(
        functools.partial(_pool_body, p0=p0, rows=rows),
        grid=(b, g),
        in_specs=[pl.BlockSpec((1, POOL_HALO, gc), lambda i, j: (i, 0, j)),
                  pl.BlockSpec((1, t, gc), lambda i, j: (i, 0, col_u // gc + j)),
                  pl.BlockSpec((1, gc, gc), lambda i, j: (j, 0, 0)),
                  pl.BlockSpec((1, 1, gc), lambda i, j: (j, 0, 0))],
        out_specs=pl.BlockSpec((1, t, gc), lambda i, j: (i, 0, j)),
        out_shape=jax.ShapeDtypeStruct((b, t, c), BF16),
        scratch_shapes=[pltpu.VMEM((POOL_HALO + t, gc), F32)],
        compiler_params=_params(("parallel", "arbitrary")),
        name="pool",
    )(hist, h3, w_grp, scale.reshape(g, 1, gc))


def _t5_bias(rel_bias, g):
    w, d = DILATED_GROUPS[g]
    dist = d * np.arange(w // d + 1)
    max_exact = NUM_BUCKETS // 2
    df = jnp.maximum(dist, 1).astype(F32)
    large = max_exact + (jnp.log(df / max_exact) / math.log(MAX_DISTANCE / max_exact)
                         * (NUM_BUCKETS - max_exact)).astype(jnp.int32)
    bucket = jnp.where(dist < max_exact, dist, jnp.minimum(large, NUM_BUCKETS - 1))
    return rel_bias[bucket][:, g * HEADS:(g + 1) * HEADS].T.astype(F32)


def _neg(rows, n):
    return jnp.full((rows, n), NEG_INF, F32)


def _toeplitz(p):
    h, n = p.shape
    b = n // 2
    return jnp.tile(p, (1, b))[:, :b * (n - 1)].reshape(h, b, n - 1)[:, :, :b]


def _band_tables(rel_bias):
    blk = ATTN_BLOCK
    tabs = []
    for g in range(N_GROUPS):
        bias = _t5_bias(rel_bias, g)
        cur = _toeplitz(jnp.concatenate([bias[:, :1], _neg(HEADS, blk), jnp.flip(bias[:, 1:blk], 1)], 1))
        prev = _toeplitz(jnp.concatenate([jnp.flip(bias[:, 1:blk + 1], 1), _neg(HEADS, blk)], 1))
        tabs.append(jnp.stack([cur, prev], axis=1))
    return jnp.stack(tabs)


def _softmax_block(scores, values):
    same = all(s.shape == scores[0].shape for s in scores)
    if same:
        m = functools.reduce(jnp.maximum, scores).max(axis=-1, keepdims=True)
    else:
        m = functools.reduce(jnp.maximum, [s.max(axis=-1, keepdims=True) for s in scores])
    ps = [jnp.exp(s - m) for s in scores]
    if same:
        l = functools.reduce(jnp.add, ps).sum(axis=-1, keepdims=True)
    else:
        l = functools.reduce(jnp.add, [p.sum(axis=-1, keepdims=True) for p in ps])
    o = functools.reduce(jnp.add, [_dot(p.astype(BF16), v.astype(BF16)) for p, v in zip(ps, values)])
    return m, l, o


def _attn_prompt_body(q0, k0, v0, q1, k1, v1, q2, k2, v2, tab_ref, o_ref, kv0, kv1, kv2,
                      oacc, macc, lacc, sbuf, pbuf, *, t):
    qkv = ((q0, k0, v0), (q1, k1, v1), (q2, k2, v2))
    scale = HEAD_DIM ** -0.5
    head = pl.program_id(1)
    per_pos = 2 * HEADS
    for (w, _), kv_ref, (_, k_ref, v_ref) in zip(DILATED_GROUPS, (kv0, kv1, kv2), qkv):
        n = min(w, t)
        kv_ref[0, pl.ds(head, n, stride=per_pos), :] = k_ref[0, pl.ds(t - n, n), :]
        kv_ref[0, pl.ds(HEADS + head, n, stride=per_pos), :] = v_ref[0, pl.ds(t - n, n), :]
    for g, (_, d) in enumerate(DILATED_GROUPS):
        q_ref, k_ref, v_ref = qkv[g]
        span = ATTN_BLOCK * d

        def rows(start, ref):
            if d == 1:
                return ref[0, pl.ds(start, ATTN_BLOCK), :]
            return ref[0, pl.ds(start, ATTN_BLOCK, stride=d), :]

        def out_rows(start):
            if d == 1:
                return pl.ds(start, ATTN_BLOCK)
            return pl.ds(start, ATTN_BLOCK, stride=d)

        blocks = [(s * span + r, s > 0) for s in range(t // span) for r in range(d)]
        cur = pl.ds(0, ATTN_BLOCK)
        prev = pl.ds(ATTN_BLOCK, ATTN_BLOCK)
        for bi, (start, has_prev) in enumerate(blocks):
            qb = rows(start, q_ref).astype(BF16)
            sbuf[bi, :, cur] = _dot_nt(qb, rows(start, k_ref).astype(BF16)) * scale + tab_ref[g, 0, 0]
            if has_prev:
                sbuf[bi, :, prev] = (_dot_nt(qb, rows(start - span, k_ref).astype(BF16)) * scale
                                     + tab_ref[g, 0, 1])
        for bi, (start, has_prev) in enumerate(blocks):
            keys = pl.ds(0, 2 * ATTN_BLOCK if has_prev else ATTN_BLOCK)
            sc = sbuf[bi, :, keys]
            m = sc.max(axis=-1, keepdims=True)
            p = jnp.exp(sc - m)
            pbuf[bi, :, keys] = p.astype(BF16)
            macc[g, out_rows(start), :] = jnp.broadcast_to(m, (ATTN_BLOCK, LANES))
            lacc[g, out_rows(start), :] = jnp.broadcast_to(p.sum(axis=-1, keepdims=True), (ATTN_BLOCK, LANES))
        for bi, (start, has_prev) in enumerate(blocks):
            o = _dot(pbuf[bi, :, cur], rows(start, v_ref).astype(BF16))
            if has_prev:
                o = o + _dot(pbuf[bi, :, prev], rows(start - span, v_ref).astype(BF16))
            oacc[g, out_rows(start), :] = o
    for r0 in range(0, t, ATTN_BLOCK):
        sl = pl.ds(r0, ATTN_BLOCK)
        ms = [macc[g, sl, :] for g in range(N_GROUPS)]
        mm = jnp.maximum(jnp.maximum(ms[0], ms[1]), ms[2])
        den = None
        num = None
        for g in range(N_GROUPS):
            c = jnp.exp(ms[g] - mm)
            dn = lacc[g, sl, :] * c
            nm = oacc[g, sl, :] * c
            den = dn if den is None else den + dn
            num = nm if num is None else num + nm
        o_ref[0, sl, :] = (num / den).astype(o_ref.dtype)


def _attn_prompt(h3, tabs, col_q, col_k, col_v):
    b, t, _ = h3.shape
    assert t % (ATTN_BLOCK * max(d for _, d in DILATED_GROUPS)) == 0
    per_pos = 2 * HEADS
    wins = [min(w, t) for w, _ in DILATED_GROUPS]

    def spec(col, g):
        blk = col // HEAD_DIM + HEADS * g
        return pl.BlockSpec((1, t, HEAD_DIM), lambda i, h, blk=blk: (i, 0, blk + h))

    in_specs = []
    for g in range(N_GROUPS):
        in_specs += [spec(col_q, g), spec(col_k, g), spec(col_v, g)]
    in_specs.append(pl.BlockSpec((N_GROUPS, 1, 2, ATTN_BLOCK, ATTN_BLOCK), lambda i, h: (0, h, 0, 0, 0)))
    outs = pl.pallas_call(
        functools.partial(_attn_prompt_body, t=t),
        grid=(b, HEADS),
        in_specs=in_specs,
        out_specs=[pl.BlockSpec((1, t, HEAD_DIM), lambda i, h: (i, 0, h))]
                  + [pl.BlockSpec((1, n * per_pos, HEAD_DIM), lambda i, h: (i, 0, 0)) for n in wins],
        out_shape=[jax.ShapeDtypeStruct((b, t, HEADS * HEAD_DIM), BF16)]
                  + [jax.ShapeDtypeStruct((b, n * per_pos, HEAD_DIM), F32) for n in wins],
        scratch_shapes=[pltpu.VMEM((N_GROUPS, t, HEAD_DIM), F32),
                        pltpu.VMEM((N_GROUPS, t, LANES), F32),
                        pltpu.VMEM((N_GROUPS, t, LANES), F32),
                        pltpu.VMEM((t // ATTN_BLOCK, ATTN_BLOCK, 2 * ATTN_BLOCK), F32),
                        pltpu.VMEM((t // ATTN_BLOCK, ATTN_BLOCK, 2 * ATTN_BLOCK), BF16)],
        compiler_params=_params(("arbitrary", "arbitrary")),
        name="attn_prompt",
    )(*([h3] * 9), tabs)
    return outs[0], [kv.reshape(b, n, 2, HEADS, HEAD_DIM) for kv, n in zip(outs[1:], wins)]


def _decode_tables(rel_bias, t_new, t_pad):
    tabs_c, tabs_n = [], []
    for g, (w, d) in enumerate(DILATED_GROUPS):
        bias = _t5_bias(rel_bias, g)
        by_dist = jnp.pad(bias[:, :, None], ((0, 0), (0, 0), (0, d - 1)), constant_values=NEG_INF)
        by_dist = by_dist.reshape(HEADS, -1)[:, :w + 1]
        rows_c, rows_n = [], []
        for t in range(t_pad):
            if t < t_new:
                rows_c.append(jnp.flip(jnp.concatenate([by_dist[:, t + 1:], _neg(HEADS, t)], 1), 1))
                rows_n.append(jnp.concatenate([jnp.flip(by_dist[:, :t + 1], 1), _neg(HEADS, t_pad - t - 1)], 1))
            else:
                rows_c.append(_neg(HEADS, w))
                rows_n.append(_neg(HEADS, t_pad))
        tabs_c.append(jnp.stack(rows_c, axis=1))
        tabs_n.append(jnp.stack(rows_n, axis=1))
    return tabs_c, tabs_n


def _attn_decode_body(q_ref, kn_ref, vn_ref, c0, c1, c2, tc0, tc1, tc2, tn0, tn1, tn2, o_ref):
    caches = (c0, c1, c2)
    tcs = (tc0, tc1, tc2)
    tns = (tn0, tn1, tn2)
    scale = HEAD_DIM ** -0.5
    per_pos = 2 * HEADS
    for h in range(HEADS):
        stats = []
        for g in range(N_GROUPS):
            col = pl.ds((g * HEADS + h) * HEAD_DIM, HEAD_DIM)
            qb = q_ref[0, :, col].astype(BF16)
            w = caches[g].shape[1] // per_pos
            kc = caches[g][0, pl.ds(h, w, stride=per_pos), :]
            vc = caches[g][0, pl.ds(HEADS + h, w, stride=per_pos), :]
            scores = [_dot_nt(qb, kc.astype(BF16)) * scale + tcs[g][h],
                      _dot_nt(qb, kn_ref[0, :, col].astype(BF16)) * scale + tns[g][h]]
            stats.append(_softmax_block(scores, [vc, vn_ref[0, :, col]]))
        mm = jnp.maximum(jnp.maximum(stats[0][0], stats[1][0]), stats[2][0])
        den = None
        num = None
        for m, l, o in stats:
            c = jnp.exp(m - mm)
            den = l * c if den is None else den + l * c
            num = o * c if num is None else num + o * c
        o_ref[0, :, pl.ds(h * HEAD_DIM, HEAD_DIM)] = (num / den).astype(o_ref.dtype)


def _attn_decode(q, kn, vn, caches, l, tabs_c, tabs_n):
    b, tp, _ = q.shape

    def full(a):
        nd = a.ndim
        return pl.BlockSpec(a.shape, lambda i, nd=nd: (0,) * nd)

    def per_batch(a):
        return pl.BlockSpec((1,) + a.shape[1:], lambda i: (i, 0, 0))

    def cache(a):
        return pl.BlockSpec((None, 1) + a.shape[2:], lambda i: (l, i, 0, 0))

    args = [q, kn, vn, *caches, *tabs_c, *tabs_n]
    in_specs = ([per_batch(a) for a in args[:3]] + [cache(a) for a in args[3:6]]
                + [full(a) for a in args[6:]])
    return pl.pallas_call(
        _attn_decode_body,
        grid=(b,),
        in_specs=in_specs,
        out_specs=pl.BlockSpec((1, tp, HEADS * HEAD_DIM), lambda i: (i, 0, 0)),
        out_shape=jax.ShapeDtypeStruct((b, tp, HEADS * HEAD_DIM), BF16),
        compiler_params=_params(("parallel",)),
        name="attn_decode",
    )(*args)


def _gla_body(q_ref, k_ref, v_ref, r_ref, low_ref, wg_ref, bg_ref, nrm_ref, s0_ref, y_ref, s_ref, st,
              *, chunk, valid, hk, hv):
    ti = pl.program_id(1)
    tt = q_ref.shape[1]

    @pl.when(ti == 0)
    def _():
        st[...] = s0_ref[0]

    wgate = wg_ref[...]
    bgate = bg_ref[...]
    nrm = nrm_ref[...]
    row = lax.broadcasted_iota(jnp.int32, (chunk, chunk), 0)
    colm = lax.broadcasted_iota(jnp.int32, (chunk, chunk), 1)
    causal = row >= colm
    tri = causal.astype(F32)
    live = lax.broadcasted_iota(jnp.int32, (chunk, 1), 0) < valid

    def step(c, carry):
        sl = pl.ds(pl.multiple_of(c * chunk, chunk), chunk)
        z = _dot(low_ref[0, sl, :].astype(BF16), wgate) + bgate
        glog = jnp.where(live, jax.nn.log_sigmoid(z) / GLA_TAU, 0.0)
        gcum = jnp.dot(tri, glog, precision=lax.Precision.HIGHEST, preferred_element_type=F32)
        glast = gcum[chunk - 1:chunk, :]
        kk = k_ref[0, sl, :]
        qg = ((q_ref[0, sl, :] * (hk ** -0.5)) * jnp.exp(gcum)).astype(BF16)
        kdn = (kk * jnp.exp(-gcum)).astype(BF16)
        kdec = (kk * jnp.exp(glast - gcum)).astype(BF16)
        decay = jnp.exp(glast)
        for h in range(HEADS):
            ks = slice(h * GLA_HK_PAD, (h + 1) * GLA_HK_PAD)
            vs = pl.ds(h * hv, hv)
            vv = v_ref[0, sl, vs].astype(BF16)
            a = jnp.where(causal, _dot_nt(qg[:, ks], kdn[:, ks]), 0.0)
            s_old = st[h]
            o = _dot_nt(qg[:, ks], s_old.astype(BF16)) + _dot(a.astype(BF16), vv)
            st[h] = decay[:, ks] * s_old + _dot_tn(vv, kdec[:, ks])
            o = o * lax.rsqrt(jnp.mean(o * o, axis=-1, keepdims=True) + RMS_EPS) * nrm
            y_ref[0, sl, vs] = (o * jax.nn.silu(r_ref[0, sl, vs])).astype(y_ref.dtype)
        return carry

    lax.fori_loop(0, tt // chunk, step, 0, unroll=min(2, tt // chunk))

    @pl.when(ti == pl.num_programs(1) - 1)
    def _():
        s_ref[0] = st[...]


GLA_TIME_TILE = 512


def _gla(h3, wgate, bgate, nrm, s0t, cols, chunk, valid, hk):
    b, t, _ = h3.shape
    hv = nrm.shape[-1]
    col_q, col_k, col_v, col_r, col_low = cols
    tt = _row_tile(t, GLA_TIME_TILE)
    kw = HEADS * GLA_HK_PAD
    vw = HEADS * hv

    def spec(col, width):
        return pl.BlockSpec((1, tt, width), lambda i, j: (i, j, col // width))

    state = pl.BlockSpec((1, HEADS, hv, GLA_HK_PAD), lambda i, j: (i, 0, 0, 0))
    return pl.pallas_call(
        functools.partial(_gla_body, chunk=chunk, valid=valid, hk=hk, hv=hv),
        grid=(b, t // tt),
        in_specs=[spec(col_q, kw), spec(col_k, kw), spec(col_v, vw), spec(col_r, vw), spec(col_low, LANES),
                  pl.BlockSpec((LANES, kw), lambda i, j: (0, 0)),
                  pl.BlockSpec((1, kw), lambda i, j: (0, 0)),
                  pl.BlockSpec((1, hv), lambda i, j: (0, 0)),
                  state],
        out_specs=[pl.BlockSpec((1, tt, vw), lambda i, j: (i, j, 0)), state],
        out_shape=[jax.ShapeDtypeStruct((b, t, vw), BF16),
                   jax.ShapeDtypeStruct((b, HEADS, hv, GLA_HK_PAD), F32)],
        scratch_shapes=[pltpu.VMEM((HEADS, hv, GLA_HK_PAD), F32)],
        compiler_params=_params(("parallel", "arbitrary")),
        name="gla",
    )(h3, h3, h3, h3, h3, wgate, bgate, nrm.reshape(1, hv), s0t)


def _pad_last(a, width):
    return jnp.pad(a, ((0, 0),) * (a.ndim - 1) + ((0, width - a.shape[-1]),))


def _pad_heads(a, hk):
    lead = a.shape[:-1]
    return _pad_last(a.reshape(lead + (HEADS, hk)), GLA_HK_PAD).reshape(lead + (HEADS * GLA_HK_PAD,))


def _layout(d_model):
    pool_w = 3 * d_model // 8
    attn_w = N_GROUPS * HEADS * HEAD_DIM
    dv = 3 * d_model // 8
    dk = dv // 2
    sections = (pool_w, attn_w, attn_w, attn_w, dk, dk, dv, GLA_RANK, dv, 3 * d_model)
    src = {}
    acc = 0
    for name, w in zip(("u", "aq", "ak", "av", "gq", "gk", "gv", "glow", "gr", "gates"), sections):
        src[name] = (acc, w)
        acc += w
    order = ("u", "aq", "ak", "av", "gv", "gr", "gq", "gk", "glow")
    widths = {"u": pool_w, "aq": attn_w, "ak": attn_w, "av": attn_w, "gv": dv, "gr": dv,
              "gq": HEADS * GLA_HK_PAD, "gk": HEADS * GLA_HK_PAD, "glow": 512}
    dst = {}
    acc = 0
    for name in order:
        dst[name] = acc
        acc += widths[name]
    return src, dst, widths, order, dk // HEADS, dv // HEADS


def _prep_weights(d_model, ffn_w_gate, ffn_w_up, ffn_w_down, w_in, pool_w, gla_w_gate, gla_b_gate,
                  w_br_pool, w_br_attn, w_br_gla, w_out):
    src, dst, widths, order, hk, hv = _layout(d_model)
    parts = []
    for name in order + ("gates",):
        off, w = src[name]
        blk = w_in[:, :, off:off + w]
        if name in ("gq", "gk"):
            blk = _pad_heads(blk, hk)
        elif name == "glow":
            blk = _pad_last(blk, widths[name])
        parts.append(blk)
    wgate = _pad_heads(jnp.pad(gla_w_gate, ((0, 0), (0, LANES - GLA_RANK), (0, 0))), hk)
    bgate = _pad_heads(gla_b_gate, hk)[:, None, :]
    return dict(
        wg=ffn_w_gate, wu=ffn_w_up,
        wd=ffn_w_down.astype(BF16),
        w_in=jnp.concatenate(parts, axis=2).astype(BF16),
        gate_col=sum(widths[n] for n in order),
        pool_w=pool_w.astype(BF16),
        gla_wgate=wgate.astype(BF16),
        gla_bgate=bgate.astype(F32),
        w_br_pool=w_br_pool.astype(BF16),
        w_br_attn=w_br_attn.astype(BF16),
        w_br_gla=w_br_gla.astype(BF16),
        w_out=w_out.astype(BF16),
    )


def _ffn_block(xp, xs, pw, l, i, gain, bias):
    hid_p, hid_s = _ffn_hidden(xp[1], xs[1], pw["wg"], pw["wu"], (l, i))
    of, ob, ofs, obs = _matmul_res_ln(hid_p, hid_s, pw["wd"], (l, i), xp[0], xs[0], gain, bias, 0.5, nkh=2)
    return (of, ob), (ofs, obs)


def _kv_rows(h3, dst, g, rows):
    gw = HEADS * HEAD_DIM
    b = h3.shape[0]
    k = h3[:, rows, dst["ak"] + g * gw: dst["ak"] + (g + 1) * gw]
    v = h3[:, rows, dst["av"] + g * gw: dst["av"] + (g + 1) * gw]
    return jnp.stack([k, v], axis=2).reshape(b, k.shape[1], 2, HEADS, HEAD_DIM)


def _mixers(h, l, bsz, t_new, decode, pw, lw, caches):
    m = h.shape[0]
    src, dst, widths, order, hk, hv = _layout(lw["ln_gain"].shape[-1])
    h3 = h.reshape(bsz, t_new, -1)
    pool_c = widths["u"]
    u = h3[:, :, dst["u"]:dst["u"] + pool_c]
    if decode:
        pool_buf, kv_bufs, gla_s = caches
        t_pad = SUBLANES
        p0 = PAST_LEN
        h3p = jnp.pad(h3, ((0, 0), (0, t_pad - t_new), (0, 0)))
        hist = pool_buf
    else:
        t_pad = t_new
        p0 = 0
        h3p = h3
        hist = jnp.zeros((bsz, POOL_KEEP, pool_c), F32)
        gla_s = jnp.zeros((bsz, HEADS, hk, hv), F32)
    hist_ext = jnp.pad(hist, ((0, 0), (POOL_HALO - POOL_KEEP, 0), (0, 0)))
    y_pool = _pool(h3p, dst["u"], hist_ext, pw["pool_w"][l], lw["pool_scale"], p0)[:, :t_new]
    new_pool = jnp.concatenate([hist, u], axis=1)[:, -POOL_KEEP:]
    aw = widths["aq"]
    if decode:
        q, kn, vn = (h3p[:, :, dst[n]:dst[n] + aw] for n in ("aq", "ak", "av"))
        flat = [c.reshape(c.shape[0], bsz, c.shape[2] * 2 * HEADS, HEAD_DIM) for c in kv_bufs]
        y_attn = _attn_decode(q, kn, vn, flat, l, *lw["decode_tabs"])[:, :t_new]
        new_kv = [_kv_rows(h3, dst, g, slice(None)) for g in range(N_GROUPS)]
    else:
        y_attn, new_kv = _attn_prompt(h3, lw["band_tabs"], dst["aq"], dst["ak"], dst["av"])
    s0t = jnp.pad(jnp.swapaxes(gla_s.astype(F32), -1, -2), ((0, 0), (0, 0), (0, 0), (0, GLA_HK_PAD - hk)))
    chunk = SUBLANES if decode else math.gcd(t_new, GLA_CHUNK)
    y_gla, st = _gla(h3p, pw["gla_wgate"][l], pw["gla_bgate"][l], lw["gla_norm"], s0t,
                     (dst["gq"], dst["gk"], dst["gv"], dst["gr"], dst["glow"]), chunk,
                     t_new if decode else chunk, hk)
    y_gla = y_gla[:, :t_new]
    new_s = jnp.swapaxes(st[..., :hk], -1, -2)
    ys = [y.reshape(m, -1) for y in (y_pool, y_attn, y_gla)]
    return ys, new_pool, new_kv, new_s


def _layer(xp, xs, l, shape_p, shape_s, pw, lw, caches):
    xp, xs = _ffn_block(xp, xs, pw, l, 0, lw["ln_gain"][0], lw["ln_bias"][0])
    h_p, h_s = _matmul(xp[1], xs[1], pw["w_in"], (l,))
    ys_p, *new_p = _mixers(h_p, l, *shape_p, False, pw, lw, None)
    ys_s, *new_s = _mixers(h_s, l, *shape_s, True, pw, lw, caches)
    merged_p, merged_s = _merge(ys_p, ys_s, [pw["w_br_pool"], pw["w_br_attn"], pw["w_br_gla"]], (l,),
                                h_p, h_s, pw["gate_col"])
    of, ob, ofs, obs = _matmul_res_ln(merged_p, merged_s, pw["w_out"], (l,), xp[0], xs[0],
                                      lw["ln_gain"][1], lw["ln_bias"][1], 1.0, nkh=1)
    xp, xs = _ffn_block((of, ob), (ofs, obs), pw, l, 1, lw["ln_gain"][2], lw["ln_bias"][2])
    return xp, xs, new_p, new_s


def kernel(x_prompt, x_sample, cache_pool, cache_kv_w128, cache_kv_w512, cache_kv_w2048, state_gla, rel_bias, ln_gain, ln_bias, ffn_w_gate, ffn_w_up, ffn_w_down, w_in, pool_w, pool_scale, gla_w_gate, gla_b_gate, gla_norm, w_br_pool, w_br_attn, w_br_gla, w_out):
    d_model = x_prompt.shape[-1]
    pw = _prep_weights(d_model, ffn_w_gate, ffn_w_up, ffn_w_down, w_in, pool_w, gla_w_gate,
                       gla_b_gate, w_br_pool, w_br_attn, w_br_gla, w_out)
    band_tabs = _band_tables(rel_bias)
    decode_tabs = _decode_tables(rel_bias, x_sample.shape[1], SUBLANES)
    kv_caches = [cache_kv_w128, cache_kv_w512, cache_kv_w2048]
    shape_p = x_prompt.shape[:2]
    shape_s = x_sample.shape[:2]
    xp = x_prompt.reshape(-1, d_model)
    xs = x_sample.reshape(-1, d_model)
    xp = (xp, xp.astype(BF16))
    xs = (xs, xs.astype(BF16))
    outs_p, outs_s = [], []
    for l in range(DEPTH):
        lw = dict(ln_gain=ln_gain[l], ln_bias=ln_bias[l], pool_scale=pool_scale[l], gla_norm=gla_norm[l],
                  band_tabs=band_tabs, decode_tabs=decode_tabs)
        xp, xs, new_p, new_s = _layer(xp, xs, l, shape_p, shape_s, pw, lw,
                                      (cache_pool[l], kv_caches, state_gla[l]))
        outs_p.append(new_p)
        outs_s.append(new_s)

    def stacked(outs):
        pools = jnp.stack([o[0] for o in outs])
        kvs = [jnp.stack([o[1][g] for o in outs]) for g in range(N_GROUPS)]
        return pools, kvs, jnp.stack([o[2] for o in outs])

    pool_p, kv_p, gla_p = stacked(outs_p)
    pool_s, rows_s, gla_s = stacked(outs_s)
    t_s = shape_s[1]
    def shift_in(c, r):
        cfg = [(0, 0, 0)] * c.ndim
        cfg[2] = (-t_s, t_s, 0)
        shifted = lax.pad(c, jnp.zeros((), c.dtype), cfg)
        return lax.dynamic_update_slice_in_dim(shifted, r.astype(c.dtype), c.shape[2] - t_s, axis=2)

    kv_s = [shift_in(c, r) for c, r in zip(kv_caches, rows_s)]
    return (xp[0].reshape(x_prompt.shape), xs[0].reshape(x_sample.shape),
            pool_p, kv_p[0], kv_p[1], kv_p[2], gla_p,
            pool_s, kv_s[0], kv_s[1], kv_s[2], gla_s)
```

```python
import functools
import math

import numpy as np
import jax
import jax.numpy as jnp
from jax import lax
from jax.experimental import pallas as pl
from jax.experimental.pallas import tpu as pltpu

DEPTH = 2
PAST_LEN = 16384
POOL_WINDOWS = (2, 4, 8, 16)
POOL_KEEP = max(POOL_WINDOWS) - 1
DILATED_GROUPS = ((128, 1), (512, 4), (2048, 16))
N_GROUPS = len(DILATED_GROUPS)
HEADS = 4
HEAD_DIM = 128
NUM_BUCKETS = 32
MAX_DISTANCE = 2048
GLA_RANK = 16
GLA_TAU = 16.0
GLA_CHUNK = 64
ALPHA = (2 * DEPTH) ** 0.25
LN_EPS = 1e-5
RMS_EPS = 1e-6
NEG_INF = -1e30

LANES = 128
SUBLANES = 8
MXU_DIM = 256
VMEM_LIMIT = 60 * 1024 * 1024

BF16 = jnp.bfloat16
F32 = jnp.float32

ROW_TILE = 2048
LN_ROW_TILE = 512
MERGE_ROW_TILE = 1024
COL_TILE = 2 * MXU_DIM
FFN_COL_TILE = MXU_DIM
POOL_ROWS = 256

GLA_HK_PAD = MXU_DIM
ATTN_BLOCK = LANES
POOL_HALO = 2 * SUBLANES


def _params(sem):
    return pltpu.CompilerParams(dimension_semantics=sem, vmem_limit_bytes=VMEM_LIMIT)


def _dot(a, b):
    return jnp.dot(a, b, preferred_element_type=F32)


def _dot_nt(a, b):
    return lax.dot_general(a, b, (((1,), (1,)), ((), ())), preferred_element_type=F32)


def _dot_tn(a, b):
    return lax.dot_general(a, b, (((0,), (0,)), ((), ())), preferred_element_type=F32)


def _row_tile(m, pref):
    return pref if m % pref == 0 else m


def _wspec(prefix, rows, cols, index):
    return pl.BlockSpec((None,) * len(prefix) + (rows, cols), lambda *g: tuple(prefix) + tuple(index(*g)))


def _side_spec(a):
    nd = len(a.shape)
    return pl.BlockSpec(tuple(a.shape), lambda *g: (0,) * nd)


def _col(j, tn):
    return pl.ds(pl.multiple_of(j * tn, tn), tn)


def _gateup_body(x_ref, xs_ref, wg_ref, wu_ref, o_ref, os_ref, *, tn):
    wg = wg_ref[...].astype(BF16)
    wu = wu_ref[...].astype(BF16)
    x = x_ref[...]
    o_ref[...] = (jax.nn.silu(_dot(x, wg)) * _dot(x, wu)).astype(o_ref.dtype)

    @pl.when(pl.program_id(0) == 0)
    def _():
        xs = xs_ref[...]
        os_ref[:, _col(pl.program_id(1), tn)] = (jax.nn.silu(_dot(xs, wg)) * _dot(xs, wu)).astype(os_ref.dtype)


def _resident_rows(tm, d):
    return pl.BlockSpec((tm, d), lambda i, j: (i, 0), pipeline_mode=pl.Buffered(1))


def _ffn_hidden(xb, xsb, wg, wu, prefix, tm_pref=ROW_TILE, tn=FFN_COL_TILE):
    m, d = xb.shape
    f = wg.shape[-1]
    tm = _row_tile(m, tm_pref)
    wspec = _wspec(prefix, d, tn, lambda i, j: (0, j))
    side_out = jax.ShapeDtypeStruct((xsb.shape[0], f), BF16)
    return pl.pallas_call(
        functools.partial(_gateup_body, tn=tn),
        grid=(m // tm, f // tn),
        in_specs=[_resident_rows(tm, d), _side_spec(xsb), wspec, wspec],
        out_specs=[pl.BlockSpec((tm, tn), lambda i, j: (i, j)), _side_spec(side_out)],
        out_shape=[jax.ShapeDtypeStruct((m, f), BF16), side_out],
        compiler_params=_params(("arbitrary", "arbitrary")),
        name="ffn_hidden",
    )(xb, xsb, wg, wu)


def _plain_body(x_ref, xs_ref, w_ref, o_ref, os_ref, *, tn):
    w = w_ref[...]
    o_ref[...] = _dot(x_ref[...], w).astype(o_ref.dtype)

    @pl.when(pl.program_id(0) == 0)
    def _():
        os_ref[:, _col(pl.program_id(1), tn)] = _dot(xs_ref[...], w).astype(os_ref.dtype)


def _matmul(xb, xsb, w, prefix, tm_pref=ROW_TILE, tn=COL_TILE, out_dtype=F32):
    m, d = xb.shape
    n = w.shape[-1]
    tm = _row_tile(m, tm_pref)
    side_out = jax.ShapeDtypeStruct((xsb.shape[0], n), out_dtype)
    return pl.pallas_call(
        functools.partial(_plain_body, tn=tn),
        grid=(m // tm, n // tn),
        in_specs=[_resident_rows(tm, d), _side_spec(xsb),
                  _wspec(prefix, d, tn, lambda i, j: (0, j))],
        out_specs=[pl.BlockSpec((tm, tn), lambda i, j: (i, j)), _side_spec(side_out)],
        out_shape=[jax.ShapeDtypeStruct((m, n), out_dtype), side_out],
        compiler_params=_params(("arbitrary", "arbitrary")),
        name="matmul",
    )(xb, xsb, w)


LN_ROWS = 128


def _res_accumulate(of_ref, col, part, x_tile, kh, nkh, coef):
    if nkh == 1:
        of_ref[:, col] = ALPHA * x_tile() + coef * part
        return

    @pl.when(kh == 0)
    def _():
        of_ref[:, col] = part

    @pl.when((kh > 0) & (kh < nkh - 1))
    def _():
        of_ref[:, col] += part

    @pl.when(kh == nkh - 1)
    def _():
        of_ref[:, col] = ALPHA * x_tile() + coef * (of_ref[:, col] + part)


def _layer_norm_rows(of_ref, ob_ref, gain, bias):
    ln_rows = math.gcd(of_ref.shape[0], LN_ROWS)

    def rows(r, carry):
        sl = pl.ds(pl.multiple_of(r * ln_rows, ln_rows), ln_rows)
        mu = jnp.mean(of_ref[sl, :], axis=-1, keepdims=True)
        c = of_ref[sl, :] - mu
        rstd = lax.rsqrt(jnp.mean(c * c, axis=-1, keepdims=True) + LN_EPS)
        y = (of_ref[sl, :] - mu) * rstd * gain + bias
        of_ref[sl, :] = y
        ob_ref[sl, :] = y.astype(ob_ref.dtype)
        return carry

    lax.fori_loop(0, of_ref.shape[0] // ln_rows, rows, 0)


def _snake_part(i, kv, nkh):
    return jnp.where(i % 2 == 0, kv, nkh - 1 - kv)


def _snake_col(i, kv, jv, nkh, nj):
    return jnp.where((i * nkh + kv) % 2 == 0, jv, nj - 1 - jv)


def _res_ln_body(a_ref, as_ref, w_ref, x_ref, xs_ref, g_ref, b_ref, of_ref, ob_ref, ofs_ref, obs_ref,
                 *, coef, nkh, nj, tn, tk):
    i = pl.program_id(0)
    kv = pl.program_id(1)
    jv = pl.program_id(2)
    col = _col(_snake_col(i, kv, jv, nkh, nj), tn)
    w = w_ref[...]
    last = (kv == nkh - 1) & (jv == nj - 1)
    _res_accumulate(of_ref, col, _dot(a_ref[...], w), lambda: x_ref[...], kv, nkh, coef)

    @pl.when(last)
    def _():
        _layer_norm_rows(of_ref, ob_ref, g_ref[...], b_ref[...])

    @pl.when(i == 0)
    def _():
        k0 = _snake_part(i, kv, nkh) * tk
        part = _dot(as_ref[:, pl.ds(pl.multiple_of(k0, LANES), tk)], w)
        _res_accumulate(ofs_ref, col, part, lambda: xs_ref[:, col], kv, nkh, coef)

    @pl.when((i == 0) & last)
    def _():
        _layer_norm_rows(ofs_ref, obs_ref, g_ref[...], b_ref[...])


def _matmul_res_ln(a, a_s, w, prefix, x, x_s, gain, bias, coef, nkh, tm_pref=LN_ROW_TILE, tn=COL_TILE):
    m, kdim = a.shape
    d = w.shape[-1]
    tm = _row_tile(m, tm_pref)
    tk = kdim // nkh
    nj = d // tn
    ms = a_s.shape[0]
    side_f = jax.ShapeDtypeStruct((ms, d), F32)
    side_b = jax.ShapeDtypeStruct((ms, d), BF16)
    vec = pl.BlockSpec((1, d), lambda i, k, j: (0, 0))
    rows = pl.BlockSpec((tm, d), lambda i, k, j: (i, 0))

    def part(i, k):
        return _snake_part(i, k, nkh)

    def col(i, k, j):
        return _snake_col(i, k, j, nkh, nj)

    def res_col(i, k, j):
        return col(i, nkh - 1, jnp.where(k == nkh - 1, j, 0))

    return pl.pallas_call(
        functools.partial(_res_ln_body, coef=coef, nkh=nkh, nj=nj, tn=tn, tk=tk),
        grid=(m // tm, nkh, nj),
        in_specs=[pl.BlockSpec((tm, tk), lambda i, k, j: (i, part(i, k))), _side_spec(a_s),
                  _wspec(prefix, tk, tn, lambda i, k, j: (part(i, k), col(i, k, j))),
                  pl.BlockSpec((tm, tn), lambda i, k, j: (i, res_col(i, k, j))),
                  _side_spec(x_s), vec, vec],
        out_specs=[rows, rows, _side_spec(side_f), _side_spec(side_b)],
        out_shape=[jax.ShapeDtypeStruct((m, d), F32), jax.ShapeDtypeStruct((m, d), BF16), side_f, side_b],
        compiler_params=_params(("arbitrary", "arbitrary", "arbitrary")),
        name="matmul_res_ln",
    )(a, a_s, w, x, x_s, gain.reshape(1, d), bias.reshape(1, d))


MERGE_ROWS = 256


def _merge_body(yp_ref, ya_ref, yg_ref, sp_ref, sa_ref, sg_ref, wp_ref, wa_ref, wg_ref,
                gp_ref, ga_ref, gg_ref, hs_ref, o_ref, os_ref, *, tn, g0, nd):
    wp = wp_ref[...]
    wa = wa_ref[...]
    wg = wg_ref[...]
    tm = o_ref.shape[0]
    step = math.gcd(tm, MERGE_ROWS)
    for r0 in range(0, tm, step):
        rows = pl.ds(r0, step)
        acc = jax.nn.sigmoid(gp_ref[rows, :]) * _dot(yp_ref[rows, :], wp)
        acc += jax.nn.sigmoid(ga_ref[rows, :]) * _dot(ya_ref[rows, :], wa)
        acc += jax.nn.sigmoid(gg_ref[rows, :]) * _dot(yg_ref[rows, :], wg)
        o_ref[rows, :] = acc.astype(o_ref.dtype)

    @pl.when(pl.program_id(0) == 0)
    def _():
        j = pl.program_id(1)
        acc_s = None
        for n, (y_ref, w) in enumerate(((sp_ref, wp), (sa_ref, wa), (sg_ref, wg))):
            term = jax.nn.sigmoid(hs_ref[:, _col(g0 + n * nd + j, tn)]) * _dot(y_ref[...], w)
            acc_s = term if acc_s is None else acc_s + term
        os_ref[:, _col(j, tn)] = acc_s.astype(os_ref.dtype)


def _merge(ys, ys_side, ws, prefix, h, h_side, gate_col, tm_pref=MERGE_ROW_TILE, tn=COL_TILE):
    m = ys[0].shape[0]
    d = ws[0].shape[-1]
    tm = _row_tile(m, tm_pref)
    nd = d // tn
    g0 = gate_col // tn
    side_out = jax.ShapeDtypeStruct((h_side.shape[0], d), BF16)

    def yspec(y):
        return pl.BlockSpec((tm, y.shape[1]), lambda i, j: (i, 0))

    def wspec(w):
        return _wspec(prefix, w.shape[-2], tn, lambda i, j: (0, j))

    def gspec(n):
        return pl.BlockSpec((tm, tn), lambda i, j: (i, g0 + n * nd + j))

    return pl.pallas_call(
        functools.partial(_merge_body, tn=tn, g0=g0, nd=nd),
        grid=(m // tm, nd),
        in_specs=([yspec(y) for y in ys] + [_side_spec(y) for y in ys_side] + [wspec(w) for w in ws]
                  + [gspec(0), gspec(1), gspec(2), _side_spec(h_side)]),
        out_specs=[pl.BlockSpec((tm, tn), lambda i, j: (i, j)), _side_spec(side_out)],
        out_shape=[jax.ShapeDtypeStruct((m, d), BF16), side_out],
        compiler_params=_params(("arbitrary", "arbitrary")),
        name="merge",
    )(*ys, *ys_side, *ws, h, h, h, h_side)


def _pool_body(hist_ref, u_ref, w_ref, s_ref, o_ref, ext, *, p0, rows):
    t = o_ref.shape[1]
    gi = pl.program_id(1)
    ext[pl.ds(0, POOL_HALO), :] = hist_ref[0]
    ext[pl.ds(POOL_HALO, t), :] = u_ref[0]
    for idx, win in enumerate(POOL_WINDOWS):
        @pl.when(gi == idx)
        def _(win=win):
            wmat = w_ref[0]
            scale = s_ref[0]
            for r0 in range(0, t, rows):
                cur = ext[pl.ds(POOL_HALO + r0, rows), :]
                tot = cur
                for s in range(1, win):
                    tot = tot + ext[pl.ds(POOL_HALO + r0 - s, rows), :]
                pos = p0 + r0 + lax.broadcasted_iota(jnp.int32, (rows, 1), 0)
                cnt = jnp.minimum(pos + 1, win).astype(F32)
                dlt = tot / cnt - cur
                y = _dot(dlt.astype(BF16), wmat) * scale
                o_ref[0, pl.ds(r0, rows), :] = y.astype(o_ref.dtype)


def _pool(h3, col_u, hist, w_grp, scale, p0):
    b, t, _ = h3.shape
    c = hist.shape[-1]
    g = len(POOL_WINDOWS)
    gc = c // g
    rows = min(t, POOL_ROWS)
    return pl.pallas_call(
        functools.partial(_pool_body, p0=p0, rows=rows),
        grid=(b, g),
        in_specs=[pl.BlockSpec((1, POOL_HALO, gc), lambda i, j: (i, 0, j)),
                  pl.BlockSpec((1, t, gc), lambda i, j: (i, 0, col_u // gc + j)),
                  pl.BlockSpec((1, gc, gc), lambda i, j: (j, 0, 0)),
                  pl.BlockSpec((1, 1, gc), lambda i, j: (j, 0, 0))],
        out_specs=pl.BlockSpec((1, t, gc), lambda i, j: (i, 0, j)),
        out_shape=jax.ShapeDtypeStruct((b, t, c), BF16),
        scratch_shapes=[pltpu.VMEM((POOL_HALO + t, gc), F32)],
        compiler_params=_params(("parallel", "arbitrary")),
        name="pool",
    )(hist, h3, w_grp, scale.reshape(g, 1, gc))


def _t5_bias(rel_bias, g):
    w, d = DILATED_GROUPS[g]
    dist = d * np.arange(w // d + 1)
    max_exact = NUM_BUCKETS // 2
    df = jnp.maximum(dist, 1).astype(F32)
    large = max_exact + (jnp.log(df / max_exact) / math.log(MAX_DISTANCE / max_exact)
                         * (NUM_BUCKETS - max_exact)).astype(jnp.int32)
    bucket = jnp.where(dist < max_exact, dist, jnp.minimum(large, NUM_BUCKETS - 1))
    return rel_bias[bucket][:, g * HEADS:(g + 1) * HEADS].T.astype(F32)


def _neg(rows, n):
    return jnp.full((rows, n), NEG_INF, F32)


def _toeplitz(p):
    h, n = p.shape
    b = n // 2
    return jnp.tile(p, (1, b))[:, :b * (n - 1)].reshape(h, b, n - 1)[:, :, :b]


def _band_tables(rel_bias):
    blk = ATTN_BLOCK
    tabs = []
    for g in range(N_GROUPS):
        bias = _t5_bias(rel_bias, g)
        cur = _toeplitz(jnp.concatenate([bias[:, :1], _neg(HEADS, blk), jnp.flip(bias[:, 1:blk], 1)], 1))
        prev = _toeplitz(jnp.concatenate([jnp.flip(bias[:, 1:blk + 1], 1), _neg(HEADS, blk)], 1))
        tabs.append(jnp.stack([cur, prev], axis=1))
    return jnp.stack(tabs)


def _softmax_block(scores, values):
    same = all(s.shape == scores[0].shape for s in scores)
    if same:
        m = functools.reduce(jnp.maximum, scores).max(axis=-1, keepdims=True)
    else:
        m = functools.reduce(jnp.maximum, [s.max(axis=-1, keepdims=True) for s in scores])
    ps = [jnp.exp(s - m) for s in scores]
    if same:
        l = functools.reduce(jnp.add, ps).sum(axis=-1, keepdims=True)
    else:
        l = functools.reduce(jnp.add, [p.sum(axis=-1, keepdims=True) for p in ps])
    o = functools.reduce(jnp.add, [_dot(p.astype(BF16), v.astype(BF16)) for p, v in zip(ps, values)])
    return m, l, o


def _attn_prompt_body(q0, k0, v0, q1, k1, v1, q2, k2, v2, tab_ref, o_ref, kv0, kv1, kv2,
                      oacc, macc, lacc, sbuf, pbuf, *, t):
    qkv = ((q0, k0, v0), (q1, k1, v1), (q2, k2, v2))
    scale = HEAD_DIM ** -0.5
    head = pl.program_id(1)
    per_pos = 2 * HEADS
    for (w, _), kv_ref, (_, k_ref, v_ref) in zip(DILATED_GROUPS, (kv0, kv1, kv2), qkv):
        n = min(w, t)
        kv_ref[0, pl.ds(head, n, stride=per_pos), :] = k_ref[0, pl.ds(t - n, n), :]
        kv_ref[0, pl.ds(HEADS + head, n, stride=per_pos), :] = v_ref[0, pl.ds(t - n, n), :]
    for g, (_, d) in enumerate(DILATED_GROUPS):
        q_ref, k_ref, v_ref = qkv[g]
        span = ATTN_BLOCK * d

        def rows(start, ref):
            if d == 1:
                return ref[0, pl.ds(start, ATTN_BLOCK), :]
            return ref[0, pl.ds(start, ATTN_BLOCK, stride=d), :]

        def out_rows(start):
            if d == 1:
                return pl.ds(start, ATTN_BLOCK)
            return pl.ds(start, ATTN_BLOCK, stride=d)

        blocks = [(s * span + r, s > 0) for s in range(t // span) for r in range(d)]
        cur = pl.ds(0, ATTN_BLOCK)
        prev = pl.ds(ATTN_BLOCK, ATTN_BLOCK)
        for bi, (start, has_prev) in enumerate(blocks):
            qb = rows(start, q_ref).astype(BF16)
            sbuf[bi, :, cur] = _dot_nt(qb, rows(start, k_ref).astype(BF16)) * scale + tab_ref[g, 0, 0]
            if has_prev:
                sbuf[bi, :, prev] = (_dot_nt(qb, rows(start - span, k_ref).astype(BF16)) * scale
                                     + tab_ref[g, 0, 1])
        for bi, (start, has_prev) in enumerate(blocks):
            keys = pl.ds(0, 2 * ATTN_BLOCK if has_prev else ATTN_BLOCK)
            sc = sbuf[bi, :, keys]
            m = sc.max(axis=-1, keepdims=True)
            p = jnp.exp(sc - m)
            pbuf[bi, :, keys] = p.astype(BF16)
            macc[g, out_rows(start), :] = jnp.broadcast_to(m, (ATTN_BLOCK, LANES))
            lacc[g, out_rows(start), :] = jnp.broadcast_to(p.sum(axis=-1, keepdims=True), (ATTN_BLOCK, LANES))
        for bi, (start, has_prev) in enumerate(blocks):
            o = _dot(pbuf[bi, :, cur], rows(start, v_ref).astype(BF16))
            if has_prev:
                o = o + _dot(pbuf[bi, :, prev], rows(start - span, v_ref).astype(BF16))
            oacc[g, out_rows(start), :] = o
    for r0 in range(0, t, ATTN_BLOCK):
        sl = pl.ds(r0, ATTN_BLOCK)
        ms = [macc[g, sl, :] for g in range(N_GROUPS)]
        mm = jnp.maximum(jnp.maximum(ms[0], ms[1]), ms[2])
        den = None
        num = None
        for g in range(N_GROUPS):
            c = jnp.exp(ms[g] - mm)
            dn = lacc[g, sl, :] * c
            nm = oacc[g, sl, :] * c
            den = dn if den is None else den + dn
            num = nm if num is None else num + nm
        o_ref[0, sl, :] = (num / den).astype(o_ref.dtype)


def _attn_prompt(h3, tabs, col_q, col_k, col_v):
    b, t, _ = h3.shape
    assert t % (ATTN_BLOCK * max(d for _, d in DILATED_GROUPS)) == 0
    per_pos = 2 * HEADS
    wins = [min(w, t) for w, _ in DILATED_GROUPS]

    def spec(col, g):
        blk = col // HEAD_DIM + HEADS * g
        return pl.BlockSpec((1, t, HEAD_DIM), lambda i, h, blk=blk: (i, 0, blk + h))

    in_specs = []
    for g in range(N_GROUPS):
        in_specs += [spec(col_q, g), spec(col_k, g), spec(col_v, g)]
    in_specs.append(pl.BlockSpec((N_GROUPS, 1, 2, ATTN_BLOCK, ATTN_BLOCK), lambda i, h: (0, h, 0, 0, 0)))
    outs = pl.pallas_call(
        functools.partial(_attn_prompt_body, t=t),
        grid=(b, HEADS),
        in_specs=in_specs,
        out_specs=[pl.BlockSpec((1, t, HEAD_DIM), lambda i, h: (i, 0, h))]
                  + [pl.BlockSpec((1, n * per_pos, HEAD_DIM), lambda i, h: (i, 0, 0)) for n in wins],
        out_shape=[jax.ShapeDtypeStruct((b, t, HEADS * HEAD_DIM), BF16)]
                  + [jax.ShapeDtypeStruct((b, n * per_pos, HEAD_DIM), F32) for n in wins],
        scratch_shapes=[pltpu.VMEM((N_GROUPS, t, HEAD_DIM), F32),
                        pltpu.VMEM((N_GROUPS, t, LANES), F32),
                        pltpu.VMEM((N_GROUPS, t, LANES), F32),
                        pltpu.VMEM((t // ATTN_BLOCK, ATTN_BLOCK, 2 * ATTN_BLOCK), F32),
                        pltpu.VMEM((t // ATTN_BLOCK, ATTN_BLOCK, 2 * ATTN_BLOCK), BF16)],
        compiler_params=_params(("arbitrary", "arbitrary")),
        name="attn_prompt",
    )(*([h3] * 9), tabs)
    return outs[0], [kv.reshape(b, n, 2, HEADS, HEAD_DIM) for kv, n in zip(outs[1:], wins)]


def _decode_tables(rel_bias, t_new, t_pad):
    tabs_c, tabs_n = [], []
    for g, (w, d) in enumerate(DILATED_GROUPS):
        bias = _t5_bias(rel_bias, g)
        by_dist = jnp.pad(bias[:, :, None], ((0, 0), (0, 0), (0, d - 1)), constant_values=NEG_INF)
        by_dist = by_dist.reshape(HEADS, -1)[:, :w + 1]
        rows_c, rows_n = [], []
        for t in range(t_pad):
            if t < t_new:
                rows_c.append(jnp.flip(jnp.concatenate([by_dist[:, t + 1:], _neg(HEADS, t)], 1), 1))
                rows_n.append(jnp.concatenate([jnp.flip(by_dist[:, :t + 1], 1), _neg(HEADS, t_pad - t - 1)], 1))
            else:
                rows_c.append(_neg(HEADS, w))
                rows_n.append(_neg(HEADS, t_pad))
        tabs_c.append(jnp.stack(rows_c, axis=1))
        tabs_n.append(jnp.stack(rows_n, axis=1))
    return tabs_c, tabs_n


def _attn_decode_body(q_ref, kn_ref, vn_ref, c0, c1, c2, tc0, tc1, tc2, tn0, tn1, tn2, o_ref):
    caches = (c0, c1, c2)
    tcs = (tc0, tc1, tc2)
    tns = (tn0, tn1, tn2)
    scale = HEAD_DIM ** -0.5
    per_pos = 2 * HEADS
    for h in range(HEADS):
        stats = []
        for g in range(N_GROUPS):
            col = pl.ds((g * HEADS + h) * HEAD_DIM, HEAD_DIM)
            qb = q_ref[0, :, col].astype(BF16)
            w = caches[g].shape[1] // per_pos
            kc = caches[g][0, pl.ds(h, w, stride=per_pos), :]
            vc = caches[g][0, pl.ds(HEADS + h, w, stride=per_pos), :]
            scores = [_dot_nt(qb, kc.astype(BF16)) * scale + tcs[g][h],
                      _dot_nt(qb, kn_ref[0, :, col].astype(BF16)) * scale + tns[g][h]]
            stats.append(_softmax_block(scores, [vc, vn_ref[0, :, col]]))
        mm = jnp.maximum(jnp.maximum(stats[0][0], stats[1][0]), stats[2][0])
        den = None
        num = None
        for m, l, o in stats:
            c = jnp.exp(m - mm)
            den = l * c if den is None else den + l * c
            num = o * c if num is None else num + o * c
        o_ref[0, :, pl.ds(h * HEAD_DIM, HEAD_DIM)] = (num / den).astype(o_ref.dtype)


def _attn_decode(q, kn, vn, caches, l, tabs_c, tabs_n):
    b, tp, _ = q.shape

    def full(a):
        nd = a.ndim
        return pl.BlockSpec(a.shape, lambda i, nd=nd: (0,) * nd)

    def per_batch(a):
        return pl.BlockSpec((1,) + a.shape[1:], lambda i: (i, 0, 0))

    def cache(a):
        return pl.BlockSpec((None, 1) + a.shape[2:], lambda i: (l, i, 0, 0))

    args = [q, kn, vn, *caches, *tabs_c, *tabs_n]
    in_specs = ([per_batch(a) for a in args[:3]] + [cache(a) for a in args[3:6]]
                + [full(a) for a in args[6:]])
    return pl.pallas_call(
        _attn_decode_body,
        grid=(b,),
        in_specs=in_specs,
        out_specs=pl.BlockSpec((1, tp, HEADS * HEAD_DIM), lambda i: (i, 0, 0)),
        out_shape=jax.ShapeDtypeStruct((b, tp, HEADS * HEAD_DIM), BF16),
        compiler_params=_params(("parallel",)),
        name="attn_decode",
    )(*args)


def _gla_body(q_ref, k_ref, v_ref, r_ref, low_ref, wg_ref, bg_ref, nrm_ref, s0_ref, y_ref, s_ref, st,
              *, chunk, valid, hk, hv):
    ti = pl.program_id(1)
    tt = q_ref.shape[1]

    @pl.when(ti == 0)
    def _():
        st[...] = s0_ref[0]

    wgate = wg_ref[...]
    bgate = bg_ref[...]
    nrm = nrm_ref[...]
    row = lax.broadcasted_iota(jnp.int32, (chunk, chunk), 0)
    colm = lax.broadcasted_iota(jnp.int32, (chunk, chunk), 1)
    causal = row >= colm
    tri = causal.astype(F32)
    live = lax.broadcasted_iota(jnp.int32, (chunk, 1), 0) < valid

    def step(c, carry):
        sl = pl.ds(pl.multiple_of(c * chunk, chunk), chunk)
        z = _dot(low_ref[0, sl, :].astype(BF16), wgate) + bgate
        glog = jnp.where(live, jax.nn.log_sigmoid(z) / GLA_TAU, 0.0)
        gcum = jnp.dot(tri, glog, precision=lax.Precision.HIGHEST, preferred_element_type=F32)
        glast = gcum[chunk - 1:chunk, :]
        kk = k_ref[0, sl, :]
        qg = ((q_ref[0, sl, :] * (hk ** -0.5)) * jnp.exp(gcum)).astype(BF16)
        kdn = (kk * jnp.exp(-gcum)).astype(BF16)
        kdec = (kk * jnp.exp(glast - gcum)).astype(BF16)
        decay = jnp.exp(glast)
        for h in range(HEADS):
            ks = slice(h * GLA_HK_PAD, (h + 1) * GLA_HK_PAD)
            vs = pl.ds(h * hv, hv)
            vv = v_ref[0, sl, vs].astype(BF16)
            a = jnp.where(causal, _dot_nt(qg[:, ks], kdn[:, ks]), 0.0)
            s_old = st[h]
            o = _dot_nt(qg[:, ks], s_old.astype(BF16)) + _dot(a.astype(BF16), vv)
            st[h] = decay[:, ks] * s_old + _dot_tn(vv, kdec[:, ks])
            o = o * lax.rsqrt(jnp.mean(o * o, axis=-1, keepdims=True) + RMS_EPS) * nrm
            y_ref[0, sl, vs] = (o * jax.nn.silu(r_ref[0, sl, vs])).astype(y_ref.dtype)
        return carry

    lax.fori_loop(0, tt // chunk, step, 0, unroll=min(2, tt // chunk))

    @pl.when(ti == pl.num_programs(1) - 1)
    def _():
        s_ref[0] = st[...]


GLA_TIME_TILE = 512


def _gla(h3, wgate, bgate, nrm, s0t, cols, chunk, valid, hk):
    b, t, _ = h3.shape
    hv = nrm.shape[-1]
    col_q, col_k, col_v, col_r, col_low = cols
    tt = _row_tile(t, GLA_TIME_TILE)
    kw = HEADS * GLA_HK_PAD
    vw = HEADS * hv

    def spec(col, width):
        return pl.BlockSpec((1, tt, width), lambda i, j: (i, j, col // width))

    state = pl.BlockSpec((1, HEADS, hv, GLA_HK_PAD), lambda i, j: (i, 0, 0, 0))
    return pl.pallas_call(
        functools.partial(_gla_body, chunk=chunk, valid=valid, hk=hk, hv=hv),
        grid=(b, t // tt),
        in_specs=[spec(col_q, kw), spec(col_k, kw), spec(col_v, vw), spec(col_r, vw), spec(col_low, LANES),
                  pl.BlockSpec((LANES, kw), lambda i, j: (0, 0)),
                  pl.BlockSpec((1, kw), lambda i, j: (0, 0)),
                  pl.BlockSpec((1, hv), lambda i, j: (0, 0)),
                  state],
        out_specs=[pl.BlockSpec((1, tt, vw), lambda i, j: (i, j, 0)), state],
        out_shape=[jax.ShapeDtypeStruct((b, t, vw), BF16),
                   jax.ShapeDtypeStruct((b, HEADS, hv, GLA_HK_PAD), F32)],
        scratch_shapes=[pltpu.VMEM((HEADS, hv, GLA_HK_PAD), F32)],
        compiler_params=_params(("parallel", "arbitrary")),
        name="gla",
    )(h3, h3, h3, h3, h3, wgate, bgate, nrm.reshape(1, hv), s0t)


def _pad_last(a, width):
    return jnp.pad(a, ((0, 0),) * (a.ndim - 1) + ((0, width - a.shape[-1]),))


def _pad_heads(a, hk):
    lead = a.shape[:-1]
    return _pad_last(a.reshape(lead + (HEADS, hk)), GLA_HK_PAD).reshape(lead + (HEADS * GLA_HK_PAD,))


def _layout(d_model):
    pool_w = 3 * d_model // 8
    attn_w = N_GROUPS * HEADS * HEAD_DIM
    dv = 3 * d_model // 8
    dk = dv // 2
    sections = (pool_w, attn_w, attn_w, attn_w, dk, dk, dv, GLA_RANK, dv, 3 * d_model)
    src = {}
    acc = 0
    for name, w in zip(("u", "aq", "ak", "av", "gq", "gk", "gv", "glow", "gr", "gates"), sections):
        src[name] = (acc, w)
        acc += w
    order = ("u", "aq", "ak", "av", "gv", "gr", "gq", "gk", "glow")
    widths = {"u": pool_w, "aq": attn_w, "ak": attn_w, "av": attn_w, "gv": dv, "gr": dv,
              "gq": HEADS * GLA_HK_PAD, "gk": HEADS * GLA_HK_PAD, "glow": COL_TILE}
    dst = {}
    acc = 0
    for name in order:
        dst[name] = acc
        acc += widths[name]
    return src, dst, widths, order, dk // HEADS, dv // HEADS


def _prep_weights(d_model, ffn_w_gate, ffn_w_up, ffn_w_down, w_in, pool_w, gla_w_gate, gla_b_gate,
                  w_br_pool, w_br_attn, w_br_gla, w_out):
    src, dst, widths, order, hk, hv = _layout(d_model)
    parts = []
    for name in order + ("gates",):
        off, w = src[name]
        blk = w_in[:, :, off:off + w]
        if name in ("gq", "gk"):
            blk = _pad_heads(blk, hk)
        elif name == "glow":
            blk = _pad_last(blk, widths[name])
        parts.append(blk)
    wgate = _pad_heads(jnp.pad(gla_w_gate, ((0, 0), (0, LANES - GLA_RANK), (0, 0))), hk)
    bgate = _pad_heads(gla_b_gate, hk)[:, None, :]
    return dict(
        wg=ffn_w_gate, wu=ffn_w_up,
        wd=ffn_w_down.astype(BF16),
        w_in=jnp.concatenate(parts, axis=2).astype(BF16),
        gate_col=sum(widths[n] for n in order),
        pool_w=pool_w.astype(BF16),
        gla_wgate=wgate.astype(BF16),
        gla_bgate=bgate,
        w_br_pool=w_br_pool.astype(BF16),
        w_br_attn=w_br_attn.astype(BF16),
        w_br_gla=w_br_gla.astype(BF16),
        w_out=w_out.astype(BF16),
    )


def _ffn_block(xp, xs, pw, l, i, gain, bias):
    hid_p, hid_s = _ffn_hidden(xp[1], xs[1], pw["wg"], pw["wu"], (l, i))
    of, ob, ofs, obs = _matmul_res_ln(hid_p, hid_s, pw["wd"], (l, i), xp[0], xs[0], gain, bias, 0.5, nkh=2)
    return (of, ob), (ofs, obs)


def _kv_rows(h3, dst, g, rows):
    gw = HEADS * HEAD_DIM
    b = h3.shape[0]
    k = h3[:, rows, dst["ak"] + g * gw: dst["ak"] + (g + 1) * gw]
    v = h3[:, rows, dst["av"] + g * gw: dst["av"] + (g + 1) * gw]
    return jnp.stack([k, v], axis=2).reshape(b, k.shape[1], 2, HEADS, HEAD_DIM)


def _mixers(h, l, bsz, t_new, decode, pw, lw, caches):
    m = h.shape[0]
    src, dst, widths, order, hk, hv = _layout(lw["ln_gain"].shape[-1])
    h3 = h.reshape(bsz, t_new, -1)
    pool_c = widths["u"]
    u = h3[:, :, dst["u"]:dst["u"] + pool_c]
    if decode:
        pool_buf, kv_bufs, gla_s = caches
        t_pad = SUBLANES
        p0 = PAST_LEN
        h3p = jnp.pad(h3, ((0, 0), (0, t_pad - t_new), (0, 0)))
        hist = pool_buf
    else:
        t_pad = t_new
        p0 = 0
        h3p = h3
        hist = jnp.zeros((bsz, POOL_KEEP, pool_c), F32)
        gla_s = jnp.zeros((bsz, HEADS, hk, hv), F32)
    hist_ext = jnp.pad(hist, ((0, 0), (POOL_HALO - POOL_KEEP, 0), (0, 0)))
    y_pool = _pool(h3p, dst["u"], hist_ext, pw["pool_w"][l], lw["pool_scale"], p0)[:, :t_new]
    new_pool = jnp.concatenate([hist, u], axis=1)[:, -POOL_KEEP:]
    aw = widths["aq"]
    if decode:
        q, kn, vn = (h3p[:, :, dst[n]:dst[n] + aw] for n in ("aq", "ak", "av"))
        flat = [c.reshape(c.shape[0], bsz, c.shape[2] * 2 * HEADS, HEAD_DIM) for c in kv_bufs]
        y_attn = _attn_decode(q, kn, vn, flat, l, *lw["decode_tabs"])[:, :t_new]
        new_kv = [_kv_rows(h3, dst, g, slice(None)) for g in range(N_GROUPS)]
    else:
        y_attn, new_kv = _attn_prompt(h3, lw["band_tabs"], dst["aq"], dst["ak"], dst["av"])
    s0t = jnp.pad(jnp.swapaxes(gla_s.astype(F32), -1, -2), ((0, 0), (0, 0), (0, 0), (0, GLA_HK_PAD - hk)))
    chunk = SUBLANES if decode else math.gcd(t_new, GLA_CHUNK)
    y_gla, st = _gla(h3p, pw["gla_wgate"][l], pw["gla_bgate"][l], lw["gla_norm"], s0t,
                     (dst["gq"], dst["gk"], dst["gv"], dst["gr"], dst["glow"]), chunk,
                     t_new if decode else chunk, hk)
    y_gla = y_gla[:, :t_new]
    new_s = jnp.swapaxes(st[..., :hk], -1, -2)
    ys = [y.reshape(m, -1) for y in (y_pool, y_attn, y_gla)]
    return ys, new_pool, new_kv, new_s


def _layer(xp, xs, l, shape_p, shape_s, pw, lw, caches):
    xp, xs = _ffn_block(xp, xs, pw, l, 0, lw["ln_gain"][0], lw["ln_bias"][0])
    h_p, h_s = _matmul(xp[1], xs[1], pw["w_in"], (l,))
    ys_p, *new_p = _mixers(h_p, l, *shape_p, False, pw, lw, None)
    ys_s, *new_s = _mixers(h_s, l, *shape_s, True, pw, lw, caches)
    merged_p, merged_s = _merge(ys_p, ys_s, [pw["w_br_pool"], pw["w_br_attn"], pw["w_br_gla"]], (l,),
                                h_p, h_s, pw["gate_col"])
    of, ob, ofs, obs = _matmul_res_ln(merged_p, merged_s, pw["w_out"], (l,), xp[0], xs[0],
                                      lw["ln_gain"][1], lw["ln_bias"][1], 1.0, nkh=1)
    xp, xs = _ffn_block((of, ob), (ofs, obs), pw, l, 1, lw["ln_gain"][2], lw["ln_bias"][2])
    return xp, xs, new_p, new_s


def kernel(x_prompt, x_sample, cache_pool, cache_kv_w128, cache_kv_w512, cache_kv_w2048, state_gla, rel_bias, ln_gain, ln_bias, ffn_w_gate, ffn_w_up, ffn_w_down, w_in, pool_w, pool_scale, gla_w_gate, gla_b_gate, gla_norm, w_br_pool, w_br_attn, w_br_gla, w_out):
    d_model = x_prompt.shape[-1]
    pw = _prep_weights(d_model, ffn_w_gate, ffn_w_up, ffn_w_down, w_in, pool_w, gla_w_gate,
                       gla_b_gate, w_br_pool, w_br_attn, w_br_gla, w_out)
    band_tabs = _band_tables(rel_bias)
    decode_tabs = _decode_tables(rel_bias, x_sample.shape[1], SUBLANES)
    kv_caches = [cache_kv_w128, cache_kv_w512, cache_kv_w2048]
    shape_p = x_prompt.shape[:2]
    shape_s = x_sample.shape[:2]
    xp = x_prompt.reshape(-1, d_model)
    xs = x_sample.reshape(-1, d_model)
    xp = (xp, xp.astype(BF16))
    xs = (xs, xs.astype(BF16))
    outs_p, outs_s = [], []
    for l in range(DEPTH):
        lw = dict(ln_gain=ln_gain[l], ln_bias=ln_bias[l], pool_scale=pool_scale[l], gla_norm=gla_norm[l],
                  band_tabs=band_tabs, decode_tabs=decode_tabs)
        xp, xs, new_p, new_s = _layer(xp, xs, l, shape_p, shape_s, pw, lw,
                                      (cache_pool[l], kv_caches, state_gla[l]))
        outs_p.append(new_p)
        outs_s.append(new_s)

    def stacked(outs):
        pools = jnp.stack([o[0] for o in outs])
        kvs = [jnp.stack([o[1][g] for o in outs]) for g in range(N_GROUPS)]
        return pools, kvs, jnp.stack([o[2] for o in outs])

    pool_p, kv_p, gla_p = stacked(outs_p)
    pool_s, rows_s, gla_s = stacked(outs_s)
    t_s = shape_s[1]
    def shift_in(c, r):
        cfg = [(0, 0, 0)] * c.ndim
        cfg[2] = (-t_s, t_s, 0)
        shifted = lax.pad(c, jnp.zeros((), c.dtype), cfg)
        return lax.dynamic_update_slice_in_dim(shifted, r.astype(c.dtype), c.shape[2] - t_s, axis=2)

    kv_s = [shift_in(c, r) for c, r in zip(kv_caches, rows_s)]
    return (xp[0].reshape(x_prompt.shape), xs[0].reshape(x_sample.shape),
            pool_p, kv_p[0], kv_p[1], kv_p[2], gla_p,
            pool_s, kv_s[0], kv_s[1], kv_s[2], gla_s)
```

```python
import functools
import math

import numpy as np
import jax
import jax.numpy as jnp
from jax import lax
from jax.experimental import pallas as pl
from jax.experimental.pallas import tpu as pltpu

DEPTH = 2
PAST_LEN = 16384
POOL_WINDOWS = (2, 4, 8, 16)
POOL_KEEP = max(POOL_WINDOWS) - 1
DILATED_GROUPS = ((128, 1), (512, 4), (2048, 16))
N_GROUPS = len(DILATED_GROUPS)
HEADS = 4
HEAD_DIM = 128
NUM_BUCKETS = 32
MAX_DISTANCE = 2048
GLA_RANK = 16
GLA_TAU = 16.0
GLA_CHUNK = 64
ALPHA = (2 * DEPTH) ** 0.25
LN_EPS = 1e-5
RMS_EPS = 1e-6
NEG_INF = -1e30

LANES = 128
SUBLANES = 8
MXU_DIM = 256
VMEM_LIMIT = 60 * 1024 * 1024

BF16 = jnp.bfloat16
F32 = jnp.float32

ROW_TILE = 2048
LN_ROW_TILE = 512
MERGE_ROW_TILE = 1024
COL_TILE = 2 * MXU_DIM
FFN_COL_TILE = MXU_DIM
POOL_ROWS = 256

GLA_HK_PAD = MXU_DIM
ATTN_BLOCK = LANES
POOL_HALO = 2 * SUBLANES


def _params(sem):
    return pltpu.CompilerParams(dimension_semantics=sem, vmem_limit_bytes=VMEM_LIMIT)


def _dot(a, b):
    return jnp.dot(a, b, preferred_element_type=F32)


def _dot_nt(a, b):
    return lax.dot_general(a, b, (((1,), (1,)), ((), ())), preferred_element_type=F32)


def _dot_tn(a, b):
    return lax.dot_general(a, b, (((0,), (0,)), ((), ())), preferred_element_type=F32)


def _row_tile(m, pref):
    return pref if m % pref == 0 else m


def _wspec(prefix, rows, cols, index):
    return pl.BlockSpec((None,) * len(prefix) + (rows, cols), lambda *g: tuple(prefix) + tuple(index(*g)))


def _side_spec(a):
    nd = len(a.shape)
    return pl.BlockSpec(tuple(a.shape), lambda *g: (0,) * nd)


def _col(j, tn):
    return pl.ds(pl.multiple_of(j * tn, tn), tn)


def _gateup_body(x_ref, xs_ref, wg_ref, wu_ref, o_ref, os_ref, *, tn):
    wg = wg_ref[...].astype(BF16)
    wu = wu_ref[...].astype(BF16)
    x = x_ref[...]
    o_ref[...] = (jax.nn.silu(_dot(x, wg)) * _dot(x, wu)).astype(o_ref.dtype)

    @pl.when(pl.program_id(0) == 0)
    def _():
        xs = xs_ref[...]
        os_ref[:, _col(pl.program_id(1), tn)] = (jax.nn.silu(_dot(xs, wg)) * _dot(xs, wu)).astype(os_ref.dtype)


def _resident_rows(tm, d):
    return pl.BlockSpec((tm, d), lambda i, j: (i, 0), pipeline_mode=pl.Buffered(1))


def _ffn_hidden(xb, xsb, wg, wu, prefix, tm_pref=ROW_TILE, tn=FFN_COL_TILE):
    m, d = xb.shape
    f = wg.shape[-1]
    tm = _row_tile(m, tm_pref)
    wspec = _wspec(prefix, d, tn, lambda i, j: (0, j))
    side_out = jax.ShapeDtypeStruct((xsb.shape[0], f), BF16)
    return pl.pallas_call(
        functools.partial(_gateup_body, tn=tn),
        grid=(m // tm, f // tn),
        in_specs=[_resident_rows(tm, d), _side_spec(xsb), wspec, wspec],
        out_specs=[pl.BlockSpec((tm, tn), lambda i, j: (i, j)), _side_spec(side_out)],
        out_shape=[jax.ShapeDtypeStruct((m, f), BF16), side_out],
        compiler_params=_params(("arbitrary", "arbitrary")),
        name="ffn_hidden",
    )(xb, xsb, wg, wu)


def _plain_body(x_ref, xs_ref, w_ref, o_ref, os_ref, *, tn):
    w = w_ref[...]
    o_ref[...] = _dot(x_ref[...], w).astype(o_ref.dtype)

    @pl.when(pl.program_id(0) == 0)
    def _():
        os_ref[:, _col(pl.program_id(1), tn)] = _dot(xs_ref[...], w).astype(os_ref.dtype)


def _matmul(xb, xsb, w, prefix, tm_pref=ROW_TILE, tn=COL_TILE, out_dtype=F32):
    m, d = xb.shape
    n = w.shape[-1]
    tm = _row_tile(m, tm_pref)
    side_out = jax.ShapeDtypeStruct((xsb.shape[0], n), out_dtype)
    return pl.pallas_call(
        functools.partial(_plain_body, tn=tn),
        grid=(m // tm, n // tn),
        in_specs=[_resident_rows(tm, d), _side_spec(xsb),
                  _wspec(prefix, d, tn, lambda i, j: (0, j))],
        out_specs=[pl.BlockSpec((tm, tn), lambda i, j: (i, j)), _side_spec(side_out)],
        out_shape=[jax.ShapeDtypeStruct((m, n), out_dtype), side_out],
        compiler_params=_params(("arbitrary", "arbitrary")),
        name="matmul",
    )(xb, xsb, w)


LN_ROWS = 128


def _res_accumulate(of_ref, col, part, x_tile, kh, nkh, coef):
    if nkh == 1:
        of_ref[:, col] = ALPHA * x_tile() + coef * part
        return

    @pl.when(kh == 0)
    def _():
        of_ref[:, col] = part

    @pl.when((kh > 0) & (kh < nkh - 1))
    def _():
        of_ref[:, col] += part

    @pl.when(kh == nkh - 1)
    def _():
        of_ref[:, col] = ALPHA * x_tile() + coef * (of_ref[:, col] + part)


def _layer_norm_rows(of_ref, ob_ref, gain, bias):
    ln_rows = math.gcd(of_ref.shape[0], LN_ROWS)

    def rows(r, carry):
        sl = pl.ds(pl.multiple_of(r * ln_rows, ln_rows), ln_rows)
        mu = jnp.mean(of_ref[sl, :], axis=-1, keepdims=True)
        c = of_ref[sl, :] - mu
        rstd = lax.rsqrt(jnp.mean(c * c, axis=-1, keepdims=True) + LN_EPS)
        y = (of_ref[sl, :] - mu) * rstd * gain + bias
        of_ref[sl, :] = y
        ob_ref[sl, :] = y.astype(ob_ref.dtype)
        return carry

    lax.fori_loop(0, of_ref.shape[0] // ln_rows, rows, 0)


def _res_ln_body(a_ref, as_ref, w_ref, x_ref, xs_ref, g_ref, b_ref, of_ref, ob_ref, ofs_ref, obs_ref,
                 *, coef, nkh, nj, tn, tk):
    i = pl.program_id(0)
    kh = pl.program_id(1)
    j = pl.program_id(2)
    col = _col(j, tn)
    w = w_ref[...]
    last = (kh == nkh - 1) & (j == nj - 1)
    _res_accumulate(of_ref, col, _dot(a_ref[...], w), lambda: x_ref[...], kh, nkh, coef)

    @pl.when(last)
    def _():
        _layer_norm_rows(of_ref, ob_ref, g_ref[...], b_ref[...])

    @pl.when(i == 0)
    def _():
        part = _dot(as_ref[:, pl.ds(pl.multiple_of(kh * tk, LANES), tk)], w)
        _res_accumulate(ofs_ref, col, part, lambda: xs_ref[:, col], kh, nkh, coef)

    @pl.when((i == 0) & last)
    def _():
        _layer_norm_rows(ofs_ref, obs_ref, g_ref[...], b_ref[...])


def _matmul_res_ln(a, a_s, w, prefix, x, x_s, gain, bias, coef, nkh, tm_pref=LN_ROW_TILE, tn=COL_TILE):
    m, kdim = a.shape
    d = w.shape[-1]
    tm = _row_tile(m, tm_pref)
    tk = kdim // nkh
    nj = d // tn
    ms = a_s.shape[0]
    side_f = jax.ShapeDtypeStruct((ms, d), F32)
    side_b = jax.ShapeDtypeStruct((ms, d), BF16)
    vec = pl.BlockSpec((1, d), lambda i, k, j: (0, 0))
    rows = pl.BlockSpec((tm, d), lambda i, k, j: (i, 0))

    def res_col(i, k, j):
        return jnp.where(k == nkh - 1, j, 0)

    return pl.pallas_call(
        functools.partial(_res_ln_body, coef=coef, nkh=nkh, nj=nj, tn=tn, tk=tk),
        grid=(m // tm, nkh, nj),
        in_specs=[pl.BlockSpec((tm, tk), lambda i, k, j: (i, k)), _side_spec(a_s),
                  _wspec(prefix, tk, tn, lambda i, k, j: (k, j)),
                  pl.BlockSpec((tm, tn), lambda i, k, j: (i, res_col(i, k, j))),
                  _side_spec(x_s), vec, vec],
        out_specs=[rows, rows, _side_spec(side_f), _side_spec(side_b)],
        out_shape=[jax.ShapeDtypeStruct((m, d), F32), jax.ShapeDtypeStruct((m, d), BF16), side_f, side_b],
        compiler_params=_params(("arbitrary", "arbitrary", "arbitrary")),
        name="matmul_res_ln",
    )(a, a_s, w, x, x_s, gain.reshape(1, d), bias.reshape(1, d))


MERGE_ROWS = 256


def _merge_body(yp_ref, ya_ref, yg_ref, sp_ref, sa_ref, sg_ref, wp_ref, wa_ref, wg_ref,
                gp_ref, ga_ref, gg_ref, hs_ref, o_ref, os_ref, *, tn, g0, nd):
    wp = wp_ref[...]
    wa = wa_ref[...]
    wg = wg_ref[...]
    tm = o_ref.shape[0]
    step = math.gcd(tm, MERGE_ROWS)
    for r0 in range(0, tm, step):
        rows = pl.ds(r0, step)
        acc = jax.nn.sigmoid(gp_ref[rows, :]) * _dot(yp_ref[rows, :], wp)
        acc += jax.nn.sigmoid(ga_ref[rows, :]) * _dot(ya_ref[rows, :], wa)
        acc += jax.nn.sigmoid(gg_ref[rows, :]) * _dot(yg_ref[rows, :], wg)
        o_ref[rows, :] = acc.astype(o_ref.dtype)

    @pl.when(pl.program_id(0) == 0)
    def _():
        j = pl.program_id(1)
        acc_s = None
        for n, (y_ref, w) in enumerate(((sp_ref, wp), (sa_ref, wa), (sg_ref, wg))):
            term = jax.nn.sigmoid(hs_ref[:, _col(g0 + n * nd + j, tn)]) * _dot(y_ref[...], w)
            acc_s = term if acc_s is None else acc_s + term
        os_ref[:, _col(j, tn)] = acc_s.astype(os_ref.dtype)


def _merge(ys, ys_side, ws, prefix, h, h_side, gate_col, tm_pref=MERGE_ROW_TILE, tn=COL_TILE):
    m = ys[0].shape[0]
    d = ws[0].shape[-1]
    tm = _row_tile(m, tm_pref)
    nd = d // tn
    g0 = gate_col // tn
    side_out = jax.ShapeDtypeStruct((h_side.shape[0], d), BF16)

    def yspec(y):
        return pl.BlockSpec((tm, y.shape[1]), lambda i, j: (i, 0))

    def wspec(w):
        return _wspec(prefix, w.shape[-2], tn, lambda i, j: (0, j))

    def gspec(n):
        return pl.BlockSpec((tm, tn), lambda i, j: (i, g0 + n * nd + j))

    return pl.pallas_call(
        functools.partial(_merge_body, tn=tn, g0=g0, nd=nd),
        grid=(m // tm, nd),
        in_specs=([yspec(y) for y in ys] + [_side_spec(y) for y in ys_side] + [wspec(w) for w in ws]
                  + [gspec(0), gspec(1), gspec(2), _side_spec(h_side)]),
        out_specs=[pl.BlockSpec((tm, tn), lambda i, j: (i, j)), _side_spec(side_out)],
        out_shape=[jax.ShapeDtypeStruct((m, d), BF16), side_out],
        compiler_params=_params(("arbitrary", "arbitrary")),
        name="merge",
    )(*ys, *ys_side, *ws, h, h, h, h_side)


def _pool_body(hist_ref, u_ref, w_ref, s_ref, o_ref, ext, *, p0, rows):
    t = o_ref.shape[1]
    gi = pl.program_id(1)
    ext[pl.ds(0, POOL_HALO), :] = hist_ref[0]
    ext[pl.ds(POOL_HALO, t), :] = u_ref[0]
    for idx, win in enumerate(POOL_WINDOWS):
        @pl.when(gi == idx)
        def _(win=win):
            wmat = w_ref[0]
            scale = s_ref[0]
            for r0 in range(0, t, rows):
                cur = ext[pl.ds(POOL_HALO + r0, rows), :]
                tot = cur
                for s in range(1, win):
                    tot = tot + ext[pl.ds(POOL_HALO + r0 - s, rows), :]
                pos = p0 + r0 + lax.broadcasted_iota(jnp.int32, (rows, 1), 0)
                cnt = jnp.minimum(pos + 1, win).astype(F32)
                dlt = tot / cnt - cur
                y = _dot(dlt.astype(BF16), wmat) * scale
                o_ref[0, pl.ds(r0, rows), :] = y.astype(o_ref.dtype)


def _pool(h3, col_u, hist, w_grp, scale, p0):
    b, t, _ = h3.shape
    c = hist.shape[-1]
    g = len(POOL_WINDOWS)
    gc = c // g
    rows = min(t, POOL_ROWS)
    return pl.pallas_call(
        functools.partial(_pool_body, p0=p0, rows=rows),
        grid=(b, g),
        in_specs=[pl.BlockSpec((1, POOL_HALO, gc), lambda i, j: (i, 0, j)),
                  pl.BlockSpec((1, t, gc), lambda i, j: (i, 0, col_u // gc + j)),
                  pl.BlockSpec((1, gc, gc), lambda i, j: (j, 0, 0)),
                  pl.BlockSpec((1, 1, gc), lambda i, j: (j, 0, 0))],
        out_specs=pl.BlockSpec((1, t, gc), lambda i, j: (i, 0, j)),
        out_shape=jax.ShapeDtypeStruct((b, t, c), BF16),
        scratch_shapes=[pltpu.VMEM((POOL_HALO + t, gc), F32)],
        compiler_params=_params(("parallel", "arbitrary")),
        name="pool",
    )(hist, h3, w_grp, scale.reshape(g, 1, gc))


def _t5_bias(rel_bias, g):
    w, d = DILATED_GROUPS[g]
    dist = d * np.arange(w // d + 1)
    max_exact = NUM_BUCKETS // 2
    df = jnp.maximum(dist, 1).astype(F32)
    large = max_exact + (jnp.log(df / max_exact) / math.log(MAX_DISTANCE / max_exact)
                         * (NUM_BUCKETS - max_exact)).astype(jnp.int32)
    bucket = jnp.where(dist < max_exact, dist, jnp.minimum(large, NUM_BUCKETS - 1))
    return rel_bias[bucket][:, g * HEADS:(g + 1) * HEADS].T.astype(F32)


def _neg(rows, n):
    return jnp.full((rows, n), NEG_INF, F32)


def _toeplitz(p):
    h, n = p.shape
    b = n // 2
    return jnp.tile(p, (1, b))[:, :b * (n - 1)].reshape(h, b, n - 1)[:, :, :b]


def _band_tables(rel_bias):
    blk = ATTN_BLOCK
    tabs = []
    for g in range(N_GROUPS):
        bias = _t5_bias(rel_bias, g)
        cur = _toeplitz(jnp.concatenate([bias[:, :1], _neg(HEADS, blk), jnp.flip(bias[:, 1:blk], 1)], 1))
        prev = _toeplitz(jnp.concatenate([jnp.flip(bias[:, 1:blk + 1], 1), _neg(HEADS, blk)], 1))
        tabs.append(jnp.stack([cur, prev], axis=1))
    return jnp.stack(tabs)


def _softmax_block(scores, values):
    same = all(s.shape == scores[0].shape for s in scores)
    if same:
        m = functools.reduce(jnp.maximum, scores).max(axis=-1, keepdims=True)
    else:
        m = functools.reduce(jnp.maximum, [s.max(axis=-1, keepdims=True) for s in scores])
    ps = [jnp.exp(s - m) for s in scores]
    if same:
        l = functools.reduce(jnp.add, ps).sum(axis=-1, keepdims=True)
    else:
        l = functools.reduce(jnp.add, [p.sum(axis=-1, keepdims=True) for p in ps])
    o = functools.reduce(jnp.add, [_dot(p.astype(BF16), v.astype(BF16)) for p, v in zip(ps, values)])
    return m, l, o


def _attn_prompt_body(q0, k0, v0, q1, k1, v1, q2, k2, v2, tab_ref, o_ref, kv0, kv1, kv2,
                      oacc, macc, lacc, sbuf, pbuf, *, t):
    qkv = ((q0, k0, v0), (q1, k1, v1), (q2, k2, v2))
    scale = HEAD_DIM ** -0.5
    head = pl.program_id(1)
    per_pos = 2 * HEADS
    for (w, _), kv_ref, (_, k_ref, v_ref) in zip(DILATED_GROUPS, (kv0, kv1, kv2), qkv):
        n = min(w, t)
        kv_ref[0, pl.ds(head, n, stride=per_pos), :] = k_ref[0, pl.ds(t - n, n), :]
        kv_ref[0, pl.ds(HEADS + head, n, stride=per_pos), :] = v_ref[0, pl.ds(t - n, n), :]
    for g, (_, d) in enumerate(DILATED_GROUPS):
        q_ref, k_ref, v_ref = qkv[g]
        span = ATTN_BLOCK * d

        def rows(start, ref):
            if d == 1:
                return ref[0, pl.ds(start, ATTN_BLOCK), :]
            return ref[0, pl.ds(start, ATTN_BLOCK, stride=d), :]

        def out_rows(start):
            if d == 1:
                return pl.ds(start, ATTN_BLOCK)
            return pl.ds(start, ATTN_BLOCK, stride=d)

        blocks = [(s * span + r, s > 0) for s in range(t // span) for r in range(d)]
        cur = pl.ds(0, ATTN_BLOCK)
        prev = pl.ds(ATTN_BLOCK, ATTN_BLOCK)
        for bi, (start, has_prev) in enumerate(blocks):
            qb = rows(start, q_ref).astype(BF16)
            sbuf[bi, :, cur] = _dot_nt(qb, rows(start, k_ref).astype(BF16)) * scale + tab_ref[g, 0, 0]
            if has_prev:
                sbuf[bi, :, prev] = (_dot_nt(qb, rows(start - span, k_ref).astype(BF16)) * scale
                                     + tab_ref[g, 0, 1])
        for bi, (start, has_prev) in enumerate(blocks):
            keys = pl.ds(0, 2 * ATTN_BLOCK if has_prev else ATTN_BLOCK)
            sc = sbuf[bi, :, keys]
            m = sc.max(axis=-1, keepdims=True)
            p = jnp.exp(sc - m)
            pbuf[bi, :, keys] = p.astype(BF16)
            macc[g, out_rows(start), :] = jnp.broadcast_to(m, (ATTN_BLOCK, LANES))
            lacc[g, out_rows(start), :] = jnp.broadcast_to(p.sum(axis=-1, keepdims=True), (ATTN_BLOCK, LANES))
        for bi, (start, has_prev) in enumerate(blocks):
            o = _dot(pbuf[bi, :, cur], rows(start, v_ref).astype(BF16))
            if has_prev:
                o = o + _dot(pbuf[bi, :, prev], rows(start - span, v_ref).astype(BF16))
            oacc[g, out_rows(start), :] = o
    for r0 in range(0, t, ATTN_BLOCK):
        sl = pl.ds(r0, ATTN_BLOCK)
        ms = [macc[g, sl, :] for g in range(N_GROUPS)]
        mm = jnp.maximum(jnp.maximum(ms[0], ms[1]), ms[2])
        den = None
        num = None
        for g in range(N_GROUPS):
            c = jnp.exp(ms[g] - mm)
            dn = lacc[g, sl, :] * c
            nm = oacc[g, sl, :] * c
            den = dn if den is None else den + dn
            num = nm if num is None else num + nm
        o_ref[0, sl, :] = (num / den).astype(o_ref.dtype)


def _attn_prompt(h3, tabs, col_q, col_k, col_v):
    b, t, _ = h3.shape
    assert t % (ATTN_BLOCK * max(d for _, d in DILATED_GROUPS)) == 0
    per_pos = 2 * HEADS
    wins = [min(w, t) for w, _ in DILATED_GROUPS]

    def spec(col, g):
        blk = col // HEAD_DIM + HEADS * g
        return pl.BlockSpec((1, t, HEAD_DIM), lambda i, h, blk=blk: (i, 0, blk + h))

    in_specs = []
    for g in range(N_GROUPS):
        in_specs += [spec(col_q, g), spec(col_k, g), spec(col_v, g)]
    in_specs.append(pl.BlockSpec((N_GROUPS, 1, 2, ATTN_BLOCK, ATTN_BLOCK), lambda i, h: (0, h, 0, 0, 0)))
    outs = pl.pallas_call(
        functools.partial(_attn_prompt_body, t=t),
        grid=(b, HEADS),
        in_specs=in_specs,
        out_specs=[pl.BlockSpec((1, t, HEAD_DIM), lambda i, h: (i, 0, h))]
                  + [pl.BlockSpec((1, n * per_pos, HEAD_DIM), lambda i, h: (i, 0, 0)) for n in wins],
        out_shape=[jax.ShapeDtypeStruct((b, t, HEADS * HEAD_DIM), BF16)]
                  + [jax.ShapeDtypeStruct((b, n * per_pos, HEAD_DIM), F32) for n in wins],
        scratch_shapes=[pltpu.VMEM((N_GROUPS, t, HEAD_DIM), F32),
                        pltpu.VMEM((N_GROUPS, t, LANES), F32),
                        pltpu.VMEM((N_GROUPS, t, LANES), F32),
                        pltpu.VMEM((t // ATTN_BLOCK, ATTN_BLOCK, 2 * ATTN_BLOCK), F32),
                        pltpu.VMEM((t // ATTN_BLOCK, ATTN_BLOCK, 2 * ATTN_BLOCK), BF16)],
        compiler_params=_params(("arbitrary", "arbitrary")),
        name="attn_prompt",
    )(*([h3] * 9), tabs)
    return outs[0], [kv.reshape(b, n, 2, HEADS, HEAD_DIM) for kv, n in zip(outs[1:], wins)]


def _decode_tables(rel_bias, t_new, t_pad):
    tabs_c, tabs_n = [], []
    for g, (w, d) in enumerate(DILATED_GROUPS):
        bias = _t5_bias(rel_bias, g)
        by_dist = jnp.pad(bias[:, :, None], ((0, 0), (0, 0), (0, d - 1)), constant_values=NEG_INF)
        by_dist = by_dist.reshape(HEADS, -1)[:, :w + 1]
        rows_c, rows_n = [], []
        for t in range(t_pad):
            if t < t_new:
                rows_c.append(jnp.flip(jnp.concatenate([by_dist[:, t + 1:], _neg(HEADS, t)], 1), 1))
                rows_n.append(jnp.concatenate([jnp.flip(by_dist[:, :t + 1], 1), _neg(HEADS, t_pad - t - 1)], 1))
            else:
                rows_c.append(_neg(HEADS, w))
                rows_n.append(_neg(HEADS, t_pad))
        tabs_c.append(jnp.stack(rows_c, axis=1))
        tabs_n.append(jnp.stack(rows_n, axis=1))
    return tabs_c, tabs_n


def _attn_decode_body(q_ref, kn_ref, vn_ref, c0, c1, c2, tc0, tc1, tc2, tn0, tn1, tn2, o_ref):
    caches = (c0, c1, c2)
    tcs = (tc0, tc1, tc2)
    tns = (tn0, tn1, tn2)
    scale = HEAD_DIM ** -0.5
    per_pos = 2 * HEADS
    for h in range(HEADS):
        stats = []
        for g in range(N_GROUPS):
            col = pl.ds((g * HEADS + h) * HEAD_DIM, HEAD_DIM)
            qb = q_ref[0, :, col].astype(BF16)
            w = caches[g].shape[1] // per_pos
            kc = caches[g][0, pl.ds(h, w, stride=per_pos), :]
            vc = caches[g][0, pl.ds(HEADS + h, w, stride=per_pos), :]
            scores = [_dot_nt(qb, kc.astype(BF16)) * scale + tcs[g][h],
                      _dot_nt(qb, kn_ref[0, :, col].astype(BF16)) * scale + tns[g][h]]
            stats.append(_softmax_block(scores, [vc, vn_ref[0, :, col]]))
        mm = jnp.maximum(jnp.maximum(stats[0][0], stats[1][0]), stats[2][0])
        den = None
        num = None
        for m, l, o in stats:
            c = jnp.exp(m - mm)
            den = l * c if den is None else den + l * c
            num = o * c if num is None else num + o * c
        o_ref[0, :, pl.ds(h * HEAD_DIM, HEAD_DIM)] = (num / den).astype(o_ref.dtype)


def _attn_decode(q, kn, vn, caches, l, tabs_c, tabs_n):
    b, tp, _ = q.shape

    def full(a):
        nd = a.ndim
        return pl.BlockSpec(a.shape, lambda i, nd=nd: (0,) * nd)

    def per_batch(a):
        return pl.BlockSpec((1,) + a.shape[1:], lambda i: (i, 0, 0))

    def cache(a):
        return pl.BlockSpec((None, 1) + a.shape[2:], lambda i: (l, i, 0, 0))

    args = [q, kn, vn, *caches, *tabs_c, *tabs_n]
    in_specs = ([per_batch(a) for a in args[:3]] + [cache(a) for a in args[3:6]]
                + [full(a) for a in args[6:]])
    return pl.pallas_call(
        _attn_decode_body,
        grid=(b,),
        in_specs=in_specs,
        out_specs=pl.BlockSpec((1, tp, HEADS * HEAD_DIM), lambda i: (i, 0, 0)),
        out_shape=jax.ShapeDtypeStruct((b, tp, HEADS * HEAD_DIM), BF16),
        compiler_params=_params(("parallel",)),
        name="attn_decode",
    )(*args)


def _gla_body(q_ref, k_ref, v_ref, r_ref, low_ref, wg_ref, bg_ref, nrm_ref, s0_ref, y_ref, s_ref, st,
              *, chunk, valid, hk, hv):
    ti = pl.program_id(1)
    tt = q_ref.shape[1]

    @pl.when(ti == 0)
    def _():
        st[...] = s0_ref[0]

    wgate = wg_ref[...]
    bgate = bg_ref[...]
    nrm = nrm_ref[...]
    row = lax.broadcasted_iota(jnp.int32, (chunk, chunk), 0)
    colm = lax.broadcasted_iota(jnp.int32, (chunk, chunk), 1)
    causal = row >= colm
    tri = causal.astype(F32)
    live = lax.broadcasted_iota(jnp.int32, (chunk, 1), 0) < valid

    def step(c, carry):
        sl = pl.ds(pl.multiple_of(c * chunk, chunk), chunk)
        z = _dot(low_ref[0, sl, :].astype(BF16), wgate) + bgate
        glog = jnp.where(live, jax.nn.log_sigmoid(z) / GLA_TAU, 0.0)
        gcum = jnp.dot(tri, glog, precision=lax.Precision.HIGHEST, preferred_element_type=F32)
        glast = gcum[chunk - 1:chunk, :]
        kk = k_ref[0, sl, :]
        qg = ((q_ref[0, sl, :] * (hk ** -0.5)) * jnp.exp(gcum)).astype(BF16)
        kdn = (kk * jnp.exp(-gcum)).astype(BF16)
        kdec = (kk * jnp.exp(glast - gcum)).astype(BF16)
        decay = jnp.exp(glast)
        for h in range(HEADS):
            ks = slice(h * GLA_HK_PAD, (h + 1) * GLA_HK_PAD)
            vs = pl.ds(h * hv, hv)
            vv = v_ref[0, sl, vs].astype(BF16)
            a = jnp.where(causal, _dot_nt(qg[:, ks], kdn[:, ks]), 0.0)
            s_old = st[h]
            o = _dot_nt(qg[:, ks], s_old.astype(BF16)) + _dot(a.astype(BF16), vv)
            st[h] = decay[:, ks] * s_old + _dot_tn(vv, kdec[:, ks])
            o = o * lax.rsqrt(jnp.mean(o * o, axis=-1, keepdims=True) + RMS_EPS) * nrm
            y_ref[0, sl, vs] = (o * jax.nn.silu(r_ref[0, sl, vs])).astype(y_ref.dtype)
        return carry

    lax.fori_loop(0, tt // chunk, step, 0, unroll=min(2, tt // chunk))

    @pl.when(ti == pl.num_programs(1) - 1)
    def _():
        s_ref[0] = st[...]


GLA_TIME_TILE = 512


def _gla(h3, wgate, bgate, nrm, s0t, cols, chunk, valid, hk):
    b, t, _ = h3.shape
    hv = nrm.shape[-1]
    col_q, col_k, col_v, col_r, col_low = cols
    tt = _row_tile(t, GLA_TIME_TILE)
    kw = HEADS * GLA_HK_PAD
    vw = HEADS * hv

    def spec(col, width):
        return pl.BlockSpec((1, tt, width), lambda i, j: (i, j, col // width))

    state = pl.BlockSpec((1, HEADS, hv, GLA_HK_PAD), lambda i, j: (i, 0, 0, 0))
    return pl.pallas_call(
        functools.partial(_gla_body, chunk=chunk, valid=valid, hk=hk, hv=hv),
        grid=(b, t // tt),
        in_specs=[spec(col_q, kw), spec(col_k, kw), spec(col_v, vw), spec(col_r, vw), spec(col_low, LANES),
                  pl.BlockSpec((LANES, kw), lambda i, j: (0, 0)),
                  pl.BlockSpec((1, kw), lambda i, j: (0, 0)),
                  pl.BlockSpec((1, hv), lambda i, j: (0, 0)),
                  state],
        out_specs=[pl.BlockSpec((1, tt, vw), lambda i, j: (i, j, 0)), state],
        out_shape=[jax.ShapeDtypeStruct((b, t, vw), BF16),
                   jax.ShapeDtypeStruct((b, HEADS, hv, GLA_HK_PAD), F32)],
        scratch_shapes=[pltpu.VMEM((HEADS, hv, GLA_HK_PAD), F32)],
        compiler_params=_params(("parallel", "arbitrary")),
        name="gla",
    )(h3, h3, h3, h3, h3, wgate, bgate, nrm.reshape(1, hv), s0t)


def _pad_last(a, width):
    return jnp.pad(a, ((0, 0),) * (a.ndim - 1) + ((0, width - a.shape[-1]),))


def _pad_heads(a, hk):
    lead = a.shape[:-1]
    return _pad_last(a.reshape(lead + (HEADS, hk)), GLA_HK_PAD).reshape(lead + (HEADS * GLA_HK_PAD,))


def _layout(d_model):
    pool_w = 3 * d_model // 8
    attn_w = N_GROUPS * HEADS * HEAD_DIM
    dv = 3 * d_model // 8
    dk = dv // 2
    sections = (pool_w, attn_w, attn_w, attn_w, dk, dk, dv, GLA_RANK, dv, 3 * d_model)
    src = {}
    acc = 0
    for name, w in zip(("u", "aq", "ak", "av", "gq", "gk", "gv", "glow", "gr", "gates"), sections):
        src[name] = (acc, w)
        acc += w
    order = ("u", "aq", "ak", "av", "gv", "gr", "gq", "gk", "glow")
    widths = {"u": pool_w, "aq": attn_w, "ak": attn_w, "av": attn_w, "gv": dv, "gr": dv,
              "gq": HEADS * GLA_HK_PAD, "gk": HEADS * GLA_HK_PAD, "glow": COL_TILE}
    dst = {}
    acc = 0
    for name in order:
        dst[name] = acc
        acc += widths[name]
    return src, dst, widths, order, dk // HEADS, dv // HEADS


def _prep_weights(d_model, ffn_w_gate, ffn_w_up, ffn_w_down, w_in, pool_w, gla_w_gate, gla_b_gate,
                  w_br_pool, w_br_attn, w_br_gla, w_out):
    src, dst, widths, order, hk, hv = _layout(d_model)
    parts = []
    for name in order + ("gates",):
        off, w = src[name]
        blk = w_in[:, :, off:off + w]
        if name in ("gq", "gk"):
            blk = _pad_heads(blk, hk)
        elif name == "glow":
            blk = _pad_last(blk, widths[name])
        parts.append(blk)
    wgate = _pad_heads(jnp.pad(gla_w_gate, ((0, 0), (0, LANES - GLA_RANK), (0, 0))), hk)
    bgate = _pad_heads(gla_b_gate, hk)[:, None, :]
    return dict(
        wg=ffn_w_gate, wu=ffn_w_up,
        wd=ffn_w_down.astype(BF16),
        w_in=jnp.concatenate(parts, axis=2).astype(BF16),
        gate_col=sum(widths[n] for n in order),
        pool_w=pool_w.astype(BF16),
        gla_wgate=wgate.astype(BF16),
        gla_bgate=bgate,
        w_br_pool=w_br_pool.astype(BF16),
        w_br_attn=w_br_attn.astype(BF16),
        w_br_gla=w_br_gla.astype(BF16),
        w_out=w_out.astype(BF16),
    )


def _ffn_block(xp, xs, pw, l, i, gain, bias):
    hid_p, hid_s = _ffn_hidden(xp[1], xs[1], pw["wg"], pw["wu"], (l, i))
    of, ob, ofs, obs = _matmul_res_ln(hid_p, hid_s, pw["wd"], (l, i), xp[0], xs[0], gain, bias, 0.5, nkh=2)
    return (of, ob), (ofs, obs)


def _kv_rows(h3, dst, g, rows):
    gw = HEADS * HEAD_DIM
    b = h3.shape[0]
    k = h3[:, rows, dst["ak"] + g * gw: dst["ak"] + (g + 1) * gw]
    v = h3[:, rows, dst["av"] + g * gw: dst["av"] + (g + 1) * gw]
    return jnp.stack([k, v], axis=2).reshape(b, k.shape[1], 2, HEADS, HEAD_DIM)


def _mixers(h, l, bsz, t_new, decode, pw, lw, caches):
    m = h.shape[0]
    src, dst, widths, order, hk, hv = _layout(lw["ln_gain"].shape[-1])
    h3 = h.reshape(bsz, t_new, -1)
    pool_c = widths["u"]
    u = h3[:, :, dst["u"]:dst["u"] + pool_c]
    if decode:
        pool_buf, kv_bufs, gla_s = caches
        t_pad = SUBLANES
        p0 = PAST_LEN
        h3p = jnp.pad(h3, ((0, 0), (0, t_pad - t_new), (0, 0)))
        hist = pool_buf
    else:
        t_pad = t_new
        p0 = 0
        h3p = h3
        hist = jnp.zeros((bsz, POOL_KEEP, pool_c), F32)
        gla_s = jnp.zeros((bsz, HEADS, hk, hv), F32)
    hist_ext = jnp.pad(hist, ((0, 0), (POOL_HALO - POOL_KEEP, 0), (0, 0)))
    y_pool = _pool(h3p, dst["u"], hist_ext, pw["pool_w"][l], lw["pool_scale"], p0)[:, :t_new]
    new_pool = jnp.concatenate([hist, u], axis=1)[:, -POOL_KEEP:]
    aw = widths["aq"]
    if decode:
        q, kn, vn = (h3p[:, :, dst[n]:dst[n] + aw] for n in ("aq", "ak", "av"))
        flat = [c.reshape(c.shape[0], bsz, c.shape[2] * 2 * HEADS, HEAD_DIM) for c in kv_bufs]
        y_attn = _attn_decode(q, kn, vn, flat, l, *lw["decode_tabs"])[:, :t_new]
        new_kv = [_kv_rows(h3, dst, g, slice(None)) for g in range(N_GROUPS)]
    else:
        y_attn, new_kv = _attn_prompt(h3, lw["band_tabs"], dst["aq"], dst["ak"], dst["av"])
    s0t = jnp.pad(jnp.swapaxes(gla_s.astype(F32), -1, -2), ((0, 0), (0, 0), (0, 0), (0, GLA_HK_PAD - hk)))
    chunk = SUBLANES if decode else math.gcd(t_new, GLA_CHUNK)
    y_gla, st = _gla(h3p, pw["gla_wgate"][l], pw["gla_bgate"][l], lw["gla_norm"], s0t,
                     (dst["gq"], dst["gk"], dst["gv"], dst["gr"], dst["glow"]), chunk,
                     t_new if decode else chunk, hk)
    y_gla = y_gla[:, :t_new]
    new_s = jnp.swapaxes(st[..., :hk], -1, -2)
    ys = [y.reshape(m, -1) for y in (y_pool, y_attn, y_gla)]
    return ys, new_pool, new_kv, new_s


def _layer(xp, xs, l, shape_p, shape_s, pw, lw, caches):
    xp, xs = _ffn_block(xp, xs, pw, l, 0, lw["ln_gain"][0], lw["ln_bias"][0])
    h_p, h_s = _matmul(xp[1], xs[1], pw["w_in"], (l,))
    ys_p, *new_p = _mixers(h_p, l, *shape_p, False, pw, lw, None)
    ys_s, *new_s = _mixers(h_s, l, *shape_s, True, pw, lw, caches)
    merged_p, merged_s = _merge(ys_p, ys_s, [pw["w_br_pool"], pw["w_br_attn"], pw["w_br_gla"]], (l,),
                                h_p, h_s, pw["gate_col"])
    of, ob, ofs, obs = _matmul_res_ln(merged_p, merged_s, pw["w_out"], (l,), xp[0], xs[0],
                                      lw["ln_gain"][1], lw["ln_bias"][1], 1.0, nkh=1)
    xp, xs = _ffn_block((of, ob), (ofs, obs), pw, l, 1, lw["ln_gain"][2], lw["ln_bias"][2])
    return xp, xs, new_p, new_s


def kernel(x_prompt, x_sample, cache_pool, cache_kv_w128, cache_kv_w512, cache_kv_w2048, state_gla, rel_bias, ln_gain, ln_bias, ffn_w_gate, ffn_w_up, ffn_w_down, w_in, pool_w, pool_scale, gla_w_gate, gla_b_gate, gla_norm, w_br_pool, w_br_attn, w_br_gla, w_out):
    d_model = x_prompt.shape[-1]
    pw = _prep_weights(d_model, ffn_w_gate, ffn_w_up, ffn_w_down, w_in, pool_w, gla_w_gate,
                       gla_b_gate, w_br_pool, w_br_attn, w_br_gla, w_out)
    band_tabs = _band_tables(rel_bias)
    decode_tabs = _decode_tables(rel_bias, x_sample.shape[1], SUBLANES)
    kv_caches = [cache_kv_w128, cache_kv_w512, cache_kv_w2048]
    shape_p = x_prompt.shape[:2]
    shape_s = x_sample.shape[:2]
    xp = x_prompt.reshape(-1, d_model)
    xs = x_sample.reshape(-1, d_model)
    xp = (xp, xp.astype(BF16))
    xs = (xs, xs.astype(BF16))
    outs_p, outs_s = [], []
    for l in range(DEPTH):
        lw = dict(ln_gain=ln_gain[l], ln_bias=ln_bias[l], pool_scale=pool_scale[l], gla_norm=gla_norm[l],
                  band_tabs=band_tabs, decode_tabs=decode_tabs)
        xp, xs, new_p, new_s = _layer(xp, xs, l, shape_p, shape_s, pw, lw,
                                      (cache_pool[l], kv_caches, state_gla[l]))
        outs_p.append(new_p)
        outs_s.append(new_s)

    def stacked(outs):
        pools = jnp.stack([o[0] for o in outs])
        kvs = [jnp.stack([o[1][g] for o in outs]) for g in range(N_GROUPS)]
        return pools, kvs, jnp.stack([o[2] for o in outs])

    pool_p, kv_p, gla_p = stacked(outs_p)
    pool_s, rows_s, gla_s = stacked(outs_s)
    t_s = shape_s[1]
    def shift_in(c, r):
        cfg = [(0, 0, 0)] * c.ndim
        cfg[2] = (-t_s, t_s, 0)
        shifted = lax.pad(c, jnp.zeros((), c.dtype), cfg)
        return lax.dynamic_update_slice_in_dim(shifted, r.astype(c.dtype), c.shape[2] - t_s, axis=2)

    kv_s = [shift_in(c, r) for c, r in zip(kv_caches, rows_s)]
    return (xp[0].reshape(x_prompt.shape), xs[0].reshape(x_sample.shape),
            pool_p, kv_p[0], kv_p[1], kv_p[2], gla_p,
            pool_s, kv_s[0], kv_s[1], kv_s[2], gla_s)
```

```python
import functools
import math

import numpy as np
import jax
import jax.numpy as jnp
from jax import lax
from jax.experimental import pallas as pl
from jax.experimental.pallas import tpu as pltpu

DEPTH = 2
PAST_LEN = 16384
POOL_WINDOWS = (2, 4, 8, 16)
POOL_KEEP = max(POOL_WINDOWS) - 1
DILATED_GROUPS = ((128, 1), (512, 4), (2048, 16))
N_GROUPS = len(DILATED_GROUPS)
HEADS = 4
HEAD_DIM = 128
NUM_BUCKETS = 32
MAX_DISTANCE = 2048
GLA_RANK = 16
GLA_TAU = 16.0
GLA_CHUNK = 64
ALPHA = (2 * DEPTH) ** 0.25
LN_EPS = 1e-5
RMS_EPS = 1e-6
NEG_INF = -1e30

LANES = 128
SUBLANES = 8
MXU_DIM = 256
VMEM_LIMIT = 60 * 1024 * 1024

BF16 = jnp.bfloat16
F32 = jnp.float32

ROW_TILE = 2048
LN_ROW_TILE = 512
MERGE_ROW_TILE = 1024
COL_TILE = 2 * MXU_DIM
FFN_COL_TILE = MXU_DIM
POOL_ROWS = 256

GLA_HK_PAD = MXU_DIM
ATTN_BLOCK = LANES
POOL_HALO = 2 * SUBLANES


def _params(sem):
    return pltpu.CompilerParams(dimension_semantics=sem, vmem_limit_bytes=VMEM_LIMIT)


def _dot(a, b):
    return jnp.dot(a, b, preferred_element_type=F32)


def _dot_nt(a, b):
    return lax.dot_general(a, b, (((1,), (1,)), ((), ())), preferred_element_type=F32)


def _dot_tn(a, b):
    return lax.dot_general(a, b, (((0,), (0,)), ((), ())), preferred_element_type=F32)


def _row_tile(m, pref):
    return pref if m % pref == 0 else m


def _wspec(prefix, rows, cols, index):
    return pl.BlockSpec((None,) * len(prefix) + (rows, cols), lambda *g: tuple(prefix) + tuple(index(*g)))


def _side_spec(a):
    nd = len(a.shape)
    return pl.BlockSpec(tuple(a.shape), lambda *g: (0,) * nd)


def _col(j, tn):
    return pl.ds(pl.multiple_of(j * tn, tn), tn)


def _gateup_body(x_ref, xs_ref, wg_ref, wu_ref, o_ref, os_ref, *, tn):
    wg = wg_ref[...].astype(BF16)
    wu = wu_ref[...].astype(BF16)
    x = x_ref[...]
    o_ref[...] = (jax.nn.silu(_dot(x, wg)) * _dot(x, wu)).astype(o_ref.dtype)

    @pl.when(pl.program_id(0) == 0)
    def _():
        xs = xs_ref[...]
        os_ref[:, _col(pl.program_id(1), tn)] = (jax.nn.silu(_dot(xs, wg)) * _dot(xs, wu)).astype(os_ref.dtype)


def _resident_rows(tm, d):
    return pl.BlockSpec((tm, d), lambda i, j: (i, 0), pipeline_mode=pl.Buffered(1))


def _ffn_hidden(xb, xsb, wg, wu, prefix, tm_pref=ROW_TILE, tn=FFN_COL_TILE):
    m, d = xb.shape
    f = wg.shape[-1]
    tm = _row_tile(m, tm_pref)
    wspec = _wspec(prefix, d, tn, lambda i, j: (0, j))
    side_out = jax.ShapeDtypeStruct((xsb.shape[0], f), BF16)
    return pl.pallas_call(
        functools.partial(_gateup_body, tn=tn),
        grid=(m // tm, f // tn),
        in_specs=[_resident_rows(tm, d), _side_spec(xsb), wspec, wspec],
        out_specs=[pl.BlockSpec((tm, tn), lambda i, j: (i, j)), _side_spec(side_out)],
        out_shape=[jax.ShapeDtypeStruct((m, f), BF16), side_out],
        compiler_params=_params(("arbitrary", "arbitrary")),
        name="ffn_hidden",
    )(xb, xsb, wg, wu)


def _plain_body(x_ref, xs_ref, w_ref, o_ref, os_ref, *, tn):
    w = w_ref[...]
    o_ref[...] = _dot(x_ref[...], w).astype(o_ref.dtype)

    @pl.when(pl.program_id(0) == 0)
    def _():
        os_ref[:, _col(pl.program_id(1), tn)] = _dot(xs_ref[...], w).astype(os_ref.dtype)


def _matmul(xb, xsb, w, prefix, tm_pref=ROW_TILE, tn=COL_TILE, out_dtype=F32):
    m, d = xb.shape
    n = w.shape[-1]
    tm = _row_tile(m, tm_pref)
    side_out = jax.ShapeDtypeStruct((xsb.shape[0], n), out_dtype)
    return pl.pallas_call(
        functools.partial(_plain_body, tn=tn),
        grid=(m // tm, n // tn),
        in_specs=[_resident_rows(tm, d), _side_spec(xsb),
                  _wspec(prefix, d, tn, lambda i, j: (0, j))],
        out_specs=[pl.BlockSpec((tm, tn), lambda i, j: (i, j)), _side_spec(side_out)],
        out_shape=[jax.ShapeDtypeStruct((m, n), out_dtype), side_out],
        compiler_params=_params(("arbitrary", "arbitrary")),
        name="matmul",
    )(xb, xsb, w)


LN_ROWS = 128


def _res_accumulate(of_ref, col, part, x_tile, kh, nkh, coef):
    if nkh == 1:
        of_ref[:, col] = ALPHA * x_tile() + coef * part
        return

    @pl.when(kh == 0)
    def _():
        of_ref[:, col] = part

    @pl.when((kh > 0) & (kh < nkh - 1))
    def _():
        of_ref[:, col] += part

    @pl.when(kh == nkh - 1)
    def _():
        of_ref[:, col] = ALPHA * x_tile() + coef * (of_ref[:, col] + part)


def _layer_norm_rows(of_ref, ob_ref, gain, bias):
    ln_rows = math.gcd(of_ref.shape[0], LN_ROWS)

    def rows(r, carry):
        sl = pl.ds(pl.multiple_of(r * ln_rows, ln_rows), ln_rows)
        mu = jnp.mean(of_ref[sl, :], axis=-1, keepdims=True)
        c = of_ref[sl, :] - mu
        rstd = lax.rsqrt(jnp.mean(c * c, axis=-1, keepdims=True) + LN_EPS)
        y = (of_ref[sl, :] - mu) * rstd * gain + bias
        of_ref[sl, :] = y
        ob_ref[sl, :] = y.astype(ob_ref.dtype)
        return carry

    lax.fori_loop(0, of_ref.shape[0] // ln_rows, rows, 0)


def _res_ln_body(a_ref, as_ref, w_ref, x_ref, xs_ref, g_ref, b_ref, of_ref, ob_ref, ofs_ref, obs_ref,
                 *, coef, nkh, nj, tn, tk):
    i = pl.program_id(0)
    kh = pl.program_id(1)
    j = pl.program_id(2)
    col = _col(j, tn)
    w = w_ref[...]
    last = (kh == nkh - 1) & (j == nj - 1)
    _res_accumulate(of_ref, col, _dot(a_ref[...], w), lambda: x_ref[...], kh, nkh, coef)

    @pl.when(last)
    def _():
        _layer_norm_rows(of_ref, ob_ref, g_ref[...], b_ref[...])

    @pl.when(i == 0)
    def _():
        part = _dot(as_ref[:, pl.ds(pl.multiple_of(kh * tk, LANES), tk)], w)
        _res_accumulate(ofs_ref, col, part, lambda: xs_ref[:, col], kh, nkh, coef)

    @pl.when((i == 0) & last)
    def _():
        _layer_norm_rows(ofs_ref, obs_ref, g_ref[...], b_ref[...])


def _matmul_res_ln(a, a_s, w, prefix, x, x_s, gain, bias, coef, nkh, tm_pref=LN_ROW_TILE, tn=COL_TILE):
    m, kdim = a.shape
    d = w.shape[-1]
    tm = _row_tile(m, tm_pref)
    tk = kdim // nkh
    nj = d // tn
    ms = a_s.shape[0]
    side_f = jax.ShapeDtypeStruct((ms, d), F32)
    side_b = jax.ShapeDtypeStruct((ms, d), BF16)
    vec = pl.BlockSpec((1, d), lambda i, k, j: (0, 0))
    if tn > COL_TILE:
        rows = pl.BlockSpec((tm, d), lambda i, k, j: (i, 0), pipeline_mode=pl.Buffered(1))
    else:
        rows = pl.BlockSpec((tm, d), lambda i, k, j: (i, 0))

    def res_col(i, k, j):
        return jnp.where(k == nkh - 1, j, 0)

    return pl.pallas_call(
        functools.partial(_res_ln_body, coef=coef, nkh=nkh, nj=nj, tn=tn, tk=tk),
        grid=(m // tm, nkh, nj),
        in_specs=[pl.BlockSpec((tm, tk), lambda i, k, j: (i, k)), _side_spec(a_s),
                  _wspec(prefix, tk, tn, lambda i, k, j: (k, j)),
                  pl.BlockSpec((tm, tn), lambda i, k, j: (i, res_col(i, k, j))),
                  _side_spec(x_s), vec, vec],
        out_specs=[rows, rows, _side_spec(side_f), _side_spec(side_b)],
        out_shape=[jax.ShapeDtypeStruct((m, d), F32), jax.ShapeDtypeStruct((m, d), BF16), side_f, side_b],
        compiler_params=_params(("arbitrary", "arbitrary", "arbitrary")),
        name="matmul_res_ln",
    )(a, a_s, w, x, x_s, gain.reshape(1, d), bias.reshape(1, d))


MERGE_ROWS = 256


def _merge_body(yp_ref, ya_ref, yg_ref, sp_ref, sa_ref, sg_ref, wp_ref, wa_ref, wg_ref,
                gp_ref, ga_ref, gg_ref, hs_ref, o_ref, os_ref, *, tn, g0, nd):
    wp = wp_ref[...]
    wa = wa_ref[...]
    wg = wg_ref[...]
    tm = o_ref.shape[0]
    step = math.gcd(tm, MERGE_ROWS)
    for r0 in range(0, tm, step):
        rows = pl.ds(r0, step)
        acc = jax.nn.sigmoid(gp_ref[rows, :]) * _dot(yp_ref[rows, :], wp)
        acc += jax.nn.sigmoid(ga_ref[rows, :]) * _dot(ya_ref[rows, :], wa)
        acc += jax.nn.sigmoid(gg_ref[rows, :]) * _dot(yg_ref[rows, :], wg)
        o_ref[rows, :] = acc.astype(o_ref.dtype)

    @pl.when(pl.program_id(0) == 0)
    def _():
        j = pl.program_id(1)
        acc_s = None
        for n, (y_ref, w) in enumerate(((sp_ref, wp), (sa_ref, wa), (sg_ref, wg))):
            term = jax.nn.sigmoid(hs_ref[:, _col(g0 + n * nd + j, tn)]) * _dot(y_ref[...], w)
            acc_s = term if acc_s is None else acc_s + term
        os_ref[:, _col(j, tn)] = acc_s.astype(os_ref.dtype)


def _merge(ys, ys_side, ws, prefix, h, h_side, gate_col, tm_pref=MERGE_ROW_TILE, tn=COL_TILE):
    m = ys[0].shape[0]
    d = ws[0].shape[-1]
    tm = _row_tile(m, tm_pref)
    nd = d // tn
    g0 = gate_col // tn
    side_out = jax.ShapeDtypeStruct((h_side.shape[0], d), BF16)

    def yspec(y):
        return pl.BlockSpec((tm, y.shape[1]), lambda i, j: (i, 0))

    def wspec(w):
        return _wspec(prefix, w.shape[-2], tn, lambda i, j: (0, j))

    def gspec(n):
        return pl.BlockSpec((tm, tn), lambda i, j: (i, g0 + n * nd + j))

    return pl.pallas_call(
        functools.partial(_merge_body, tn=tn, g0=g0, nd=nd),
        grid=(m // tm, nd),
        in_specs=([yspec(y) for y in ys] + [_side_spec(y) for y in ys_side] + [wspec(w) for w in ws]
                  + [gspec(0), gspec(1), gspec(2), _side_spec(h_side)]),
        out_specs=[pl.BlockSpec((tm, tn), lambda i, j: (i, j)), _side_spec(side_out)],
        out_shape=[jax.ShapeDtypeStruct((m, d), BF16), side_out],
        compiler_params=_params(("arbitrary", "arbitrary")),
        name="merge",
    )(*ys, *ys_side, *ws, h, h, h, h_side)


def _pool_body(hist_ref, u_ref, w_ref, s_ref, o_ref, ext, *, p0, rows):
    t = o_ref.shape[1]
    gi = pl.program_id(1)
    ext[pl.ds(0, POOL_HALO), :] = hist_ref[0]
    ext[pl.ds(POOL_HALO, t), :] = u_ref[0]
    for idx, win in enumerate(POOL_WINDOWS):
        @pl.when(gi == idx)
        def _(win=win):
            wmat = w_ref[0]
            scale = s_ref[0]
            for r0 in range(0, t, rows):
                cur = ext[pl.ds(POOL_HALO + r0, rows), :]
                tot = cur
                for s in range(1, win):
                    tot = tot + ext[pl.ds(POOL_HALO + r0 - s, rows), :]
                pos = p0 + r0 + lax.broadcasted_iota(jnp.int32, (rows, 1), 0)
                cnt = jnp.minimum(pos + 1, win).astype(F32)
                dlt = tot / cnt - cur
                y = _dot(dlt.astype(BF16), wmat) * scale
                o_ref[0, pl.ds(r0, rows), :] = y.astype(o_ref.dtype)


def _pool(h3, col_u, hist, w_grp, scale, p0):
    b, t, _ = h3.shape
    c = hist.shape[-1]
    g = len(POOL_WINDOWS)
    gc = c // g
    rows = min(t, POOL_ROWS)
    return pl.pallas_call(
        functools.partial(_pool_body, p0=p0, rows=rows),
        grid=(b, g),
        in_specs=[pl.BlockSpec((1, POOL_HALO, gc), lambda i, j: (i, 0, j)),
                  pl.BlockSpec((1, t, gc), lambda i, j: (i, 0, col_u // gc + j)),
                  pl.BlockSpec((1, gc, gc), lambda i, j: (j, 0, 0)),
                  pl.BlockSpec((1, 1, gc), lambda i, j: (j, 0, 0))],
        out_specs=pl.BlockSpec((1, t, gc), lambda i, j: (i, 0, j)),
        out_shape=jax.ShapeDtypeStruct((b, t, c), BF16),
        scratch_shapes=[pltpu.VMEM((POOL_HALO + t, gc), F32)],
        compiler_params=_params(("parallel", "arbitrary")),
        name="pool",
    )(hist, h3, w_grp, scale.reshape(g, 1, gc))


def _t5_bias(rel_bias, g):
    w, d = DILATED_GROUPS[g]
    dist = d * np.arange(w // d + 1)
    max_exact = NUM_BUCKETS // 2
    df = jnp.maximum(dist, 1).astype(F32)
    large = max_exact + (jnp.log(df / max_exact) / math.log(MAX_DISTANCE / max_exact)
                         * (NUM_BUCKETS - max_exact)).astype(jnp.int32)
    bucket = jnp.where(dist < max_exact, dist, jnp.minimum(large, NUM_BUCKETS - 1))
    return rel_bias[bucket][:, g * HEADS:(g + 1) * HEADS].T.astype(F32)


def _neg(rows, n):
    return jnp.full((rows, n), NEG_INF, F32)


def _toeplitz(p):
    h, n = p.shape
    b = n // 2
    return jnp.tile(p, (1, b))[:, :b * (n - 1)].reshape(h, b, n - 1)[:, :, :b]


def _band_tables(rel_bias):
    blk = ATTN_BLOCK
    tabs = []
    for g in range(N_GROUPS):
        bias = _t5_bias(rel_bias, g)
        cur = _toeplitz(jnp.concatenate([bias[:, :1], _neg(HEADS, blk), jnp.flip(bias[:, 1:blk], 1)], 1))
        prev = _toeplitz(jnp.concatenate([jnp.flip(bias[:, 1:blk + 1], 1), _neg(HEADS, blk)], 1))
        tabs.append(jnp.stack([cur, prev], axis=1))
    return jnp.stack(tabs)


def _softmax_block(scores, values):
    same = all(s.shape == scores[0].shape for s in scores)
    if same:
        m = functools.reduce(jnp.maximum, scores).max(axis=-1, keepdims=True)
    else:
        m = functools.reduce(jnp.maximum, [s.max(axis=-1, keepdims=True) for s in scores])
    ps = [jnp.exp(s - m) for s in scores]
    if same:
        l = functools.reduce(jnp.add, ps).sum(axis=-1, keepdims=True)
    else:
        l = functools.reduce(jnp.add, [p.sum(axis=-1, keepdims=True) for p in ps])
    o = functools.reduce(jnp.add, [_dot(p.astype(BF16), v.astype(BF16)) for p, v in zip(ps, values)])
    return m, l, o


def _attn_prompt_body(q0, k0, v0, q1, k1, v1, q2, k2, v2, tab_ref, o_ref, kv0, kv1, kv2,
                      oacc, macc, lacc, sbuf, pbuf, *, t):
    qkv = ((q0, k0, v0), (q1, k1, v1), (q2, k2, v2))
    scale = HEAD_DIM ** -0.5
    head = pl.program_id(1)
    per_pos = 2 * HEADS
    for (w, _), kv_ref, (_, k_ref, v_ref) in zip(DILATED_GROUPS, (kv0, kv1, kv2), qkv):
        n = min(w, t)
        kv_ref[0, pl.ds(head, n, stride=per_pos), :] = k_ref[0, pl.ds(t - n, n), :]
        kv_ref[0, pl.ds(HEADS + head, n, stride=per_pos), :] = v_ref[0, pl.ds(t - n, n), :]
    for g, (_, d) in enumerate(DILATED_GROUPS):
        q_ref, k_ref, v_ref = qkv[g]
        span = ATTN_BLOCK * d

        def rows(start, ref):
            if d == 1:
                return ref[0, pl.ds(start, ATTN_BLOCK), :]
            return ref[0, pl.ds(start, ATTN_BLOCK, stride=d), :]

        def out_rows(start):
            if d == 1:
                return pl.ds(start, ATTN_BLOCK)
            return pl.ds(start, ATTN_BLOCK, stride=d)

        blocks = [(s * span + r, s > 0) for s in range(t // span) for r in range(d)]
        cur = pl.ds(0, ATTN_BLOCK)
        prev = pl.ds(ATTN_BLOCK, ATTN_BLOCK)
        for bi, (start, has_prev) in enumerate(blocks):
            qb = rows(start, q_ref).astype(BF16)
            sbuf[bi, :, cur] = _dot_nt(qb, rows(start, k_ref).astype(BF16)) * scale + tab_ref[g, 0, 0]
            if has_prev:
                sbuf[bi, :, prev] = (_dot_nt(qb, rows(start - span, k_ref).astype(BF16)) * scale
                                     + tab_ref[g, 0, 1])
        for bi, (start, has_prev) in enumerate(blocks):
            keys = pl.ds(0, 2 * ATTN_BLOCK if has_prev else ATTN_BLOCK)
            sc = sbuf[bi, :, keys]
            m = sc.max(axis=-1, keepdims=True)
            p = jnp.exp(sc - m)
            pbuf[bi, :, keys] = p.astype(BF16)
            macc[g, out_rows(start), :] = jnp.broadcast_to(m, (ATTN_BLOCK, LANES))
            lacc[g, out_rows(start), :] = jnp.broadcast_to(p.sum(axis=-1, keepdims=True), (ATTN_BLOCK, LANES))
        for bi, (start, has_prev) in enumerate(blocks):
            o = _dot(pbuf[bi, :, cur], rows(start, v_ref).astype(BF16))
            if has_prev:
                o = o + _dot(pbuf[bi, :, prev], rows(start - span, v_ref).astype(BF16))
            oacc[g, out_rows(start), :] = o
    for r0 in range(0, t, ATTN_BLOCK):
        sl = pl.ds(r0, ATTN_BLOCK)
        ms = [macc[g, sl, :] for g in range(N_GROUPS)]
        mm = jnp.maximum(jnp.maximum(ms[0], ms[1]), ms[2])
        den = None
        num = None
        for g in range(N_GROUPS):
            c = jnp.exp(ms[g] - mm)
            dn = lacc[g, sl, :] * c
            nm = oacc[g, sl, :] * c
            den = dn if den is None else den + dn
            num = nm if num is None else num + nm
        o_ref[0, sl, :] = (num / den).astype(o_ref.dtype)


def _attn_prompt(h3, tabs, col_q, col_k, col_v):
    b, t, _ = h3.shape
    assert t % (ATTN_BLOCK * max(d for _, d in DILATED_GROUPS)) == 0
    per_pos = 2 * HEADS
    wins = [min(w, t) for w, _ in DILATED_GROUPS]

    def spec(col, g):
        blk = col // HEAD_DIM + HEADS * g
        return pl.BlockSpec((1, t, HEAD_DIM), lambda i, h, blk=blk: (i, 0, blk + h))

    in_specs = []
    for g in range(N_GROUPS):
        in_specs += [spec(col_q, g), spec(col_k, g), spec(col_v, g)]
    in_specs.append(pl.BlockSpec((N_GROUPS, 1, 2, ATTN_BLOCK, ATTN_BLOCK), lambda i, h: (0, h, 0, 0, 0)))
    outs = pl.pallas_call(
        functools.partial(_attn_prompt_body, t=t),
        grid=(b, HEADS),
        in_specs=in_specs,
        out_specs=[pl.BlockSpec((1, t, HEAD_DIM), lambda i, h: (i, 0, h))]
                  + [pl.BlockSpec((1, n * per_pos, HEAD_DIM), lambda i, h: (i, 0, 0)) for n in wins],
        out_shape=[jax.ShapeDtypeStruct((b, t, HEADS * HEAD_DIM), BF16)]
                  + [jax.ShapeDtypeStruct((b, n * per_pos, HEAD_DIM), F32) for n in wins],
        scratch_shapes=[pltpu.VMEM((N_GROUPS, t, HEAD_DIM), F32),
                        pltpu.VMEM((N_GROUPS, t, LANES), F32),
                        pltpu.VMEM((N_GROUPS, t, LANES), F32),
                        pltpu.VMEM((t // ATTN_BLOCK, ATTN_BLOCK, 2 * ATTN_BLOCK), F32),
                        pltpu.VMEM((t // ATTN_BLOCK, ATTN_BLOCK, 2 * ATTN_BLOCK), BF16)],
        compiler_params=_params(("arbitrary", "arbitrary")),
        name="attn_prompt",
    )(*([h3] * 9), tabs)
    return outs[0], [kv.reshape(b, n, 2, HEADS, HEAD_DIM) for kv, n in zip(outs[1:], wins)]


def _decode_tables(rel_bias, t_new, t_pad):
    tabs_c, tabs_n = [], []
    for g, (w, d) in enumerate(DILATED_GROUPS):
        bias = _t5_bias(rel_bias, g)
        by_dist = jnp.pad(bias[:, :, None], ((0, 0), (0, 0), (0, d - 1)), constant_values=NEG_INF)
        by_dist = by_dist.reshape(HEADS, -1)[:, :w + 1]
        rows_c, rows_n = [], []
        for t in range(t_pad):
            if t < t_new:
                rows_c.append(jnp.flip(jnp.concatenate([by_dist[:, t + 1:], _neg(HEADS, t)], 1), 1))
                rows_n.append(jnp.concatenate([jnp.flip(by_dist[:, :t + 1], 1), _neg(HEADS, t_pad - t - 1)], 1))
            else:
                rows_c.append(_neg(HEADS, w))
                rows_n.append(_neg(HEADS, t_pad))
        tabs_c.append(jnp.stack(rows_c, axis=1))
        tabs_n.append(jnp.stack(rows_n, axis=1))
    return tabs_c, tabs_n


def _attn_decode_body(q_ref, kn_ref, vn_ref, c0, c1, c2, tc0, tc1, tc2, tn0, tn1, tn2, o_ref):
    caches = (c0, c1, c2)
    tcs = (tc0, tc1, tc2)
    tns = (tn0, tn1, tn2)
    scale = HEAD_DIM ** -0.5
    per_pos = 2 * HEADS
    for h in range(HEADS):
        stats = []
        for g in range(N_GROUPS):
            col = pl.ds((g * HEADS + h) * HEAD_DIM, HEAD_DIM)
            qb = q_ref[0, :, col].astype(BF16)
            w = caches[g].shape[1] // per_pos
            kc = caches[g][0, pl.ds(h, w, stride=per_pos), :]
            vc = caches[g][0, pl.ds(HEADS + h, w, stride=per_pos), :]
            scores = [_dot_nt(qb, kc.astype(BF16)) * scale + tcs[g][h],
                      _dot_nt(qb, kn_ref[0, :, col].astype(BF16)) * scale + tns[g][h]]
            stats.append(_softmax_block(scores, [vc, vn_ref[0, :, col]]))
        mm = jnp.maximum(jnp.maximum(stats[0][0], stats[1][0]), stats[2][0])
        den = None
        num = None
        for m, l, o in stats:
            c = jnp.exp(m - mm)
            den = l * c if den is None else den + l * c
            num = o * c if num is None else num + o * c
        o_ref[0, :, pl.ds(h * HEAD_DIM, HEAD_DIM)] = (num / den).astype(o_ref.dtype)


def _attn_decode(q, kn, vn, caches, l, tabs_c, tabs_n):
    b, tp, _ = q.shape

    def full(a):
        nd = a.ndim
        return pl.BlockSpec(a.shape, lambda i, nd=nd: (0,) * nd)

    def per_batch(a):
        return pl.BlockSpec((1,) + a.shape[1:], lambda i: (i, 0, 0))

    def cache(a):
        return pl.BlockSpec((None, 1) + a.shape[2:], lambda i: (l, i, 0, 0))

    args = [q, kn, vn, *caches, *tabs_c, *tabs_n]
    in_specs = ([per_batch(a) for a in args[:3]] + [cache(a) for a in args[3:6]]
                + [full(a) for a in args[6:]])
    return pl.pallas_call(
        _attn_decode_body,
        grid=(b,),
        in_specs=in_specs,
        out_specs=pl.BlockSpec((1, tp, HEADS * HEAD_DIM), lambda i: (i, 0, 0)),
        out_shape=jax.ShapeDtypeStruct((b, tp, HEADS * HEAD_DIM), BF16),
        compiler_params=_params(("parallel",)),
        name="attn_decode",
    )(*args)


def _gla_body(q_ref, k_ref, v_ref, r_ref, low_ref, wg_ref, bg_ref, nrm_ref, s0_ref, y_ref, s_ref, st,
              *, chunk, valid, hk, hv):
    ti = pl.program_id(1)
    tt = q_ref.shape[1]

    @pl.when(ti == 0)
    def _():
        st[...] = s0_ref[0]

    wgate = wg_ref[...]
    bgate = bg_ref[...]
    nrm = nrm_ref[...]
    row = lax.broadcasted_iota(jnp.int32, (chunk, chunk), 0)
    colm = lax.broadcasted_iota(jnp.int32, (chunk, chunk), 1)
    causal = row >= colm
    tri = causal.astype(F32)
    live = lax.broadcasted_iota(jnp.int32, (chunk, 1), 0) < valid

    def step(c, carry):
        sl = pl.ds(pl.multiple_of(c * chunk, chunk), chunk)
        z = _dot(low_ref[0, sl, :].astype(BF16), wgate) + bgate
        glog = jnp.where(live, jax.nn.log_sigmoid(z) / GLA_TAU, 0.0)
        gcum = jnp.dot(tri, glog, precision=lax.Precision.HIGHEST, preferred_element_type=F32)
        glast = gcum[chunk - 1:chunk, :]
        kk = k_ref[0, sl, :]
        qg = ((q_ref[0, sl, :] * (hk ** -0.5)) * jnp.exp(gcum)).astype(BF16)
        kdn = (kk * jnp.exp(-gcum)).astype(BF16)
        kdec = (kk * jnp.exp(glast - gcum)).astype(BF16)
        decay = jnp.exp(glast)
        for h in range(HEADS):
            ks = slice(h * GLA_HK_PAD, (h + 1) * GLA_HK_PAD)
            vs = pl.ds(h * hv, hv)
            vv = v_ref[0, sl, vs].astype(BF16)
            a = jnp.where(causal, _dot_nt(qg[:, ks], kdn[:, ks]), 0.0)
            s_old = st[h]
            o = _dot_nt(qg[:, ks], s_old.astype(BF16)) + _dot(a.astype(BF16), vv)
            st[h] = decay[:, ks] * s_old + _dot_tn(vv, kdec[:, ks])
            o = o * lax.rsqrt(jnp.mean(o * o, axis=-1, keepdims=True) + RMS_EPS) * nrm
            y_ref[0, sl, vs] = (o * jax.nn.silu(r_ref[0, sl, vs])).astype(y_ref.dtype)
        return carry

    lax.fori_loop(0, tt // chunk, step, 0, unroll=min(2, tt // chunk))

    @pl.when(ti == pl.num_programs(1) - 1)
    def _():
        s_ref[0] = st[...]


GLA_TIME_TILE = 512


def _gla(h3, wgate, bgate, nrm, s0t, cols, chunk, valid, hk):
    b, t, _ = h3.shape
    hv = nrm.shape[-1]
    col_q, col_k, col_v, col_r, col_low = cols
    tt = _row_tile(t, GLA_TIME_TILE)
    kw = HEADS * GLA_HK_PAD
    vw = HEADS * hv

    def spec(col, width):
        return pl.BlockSpec((1, tt, width), lambda i, j: (i, j, col // width))

    state = pl.BlockSpec((1, HEADS, hv, GLA_HK_PAD), lambda i, j: (i, 0, 0, 0))
    return pl.pallas_call(
        functools.partial(_gla_body, chunk=chunk, valid=valid, hk=hk, hv=hv),
        grid=(b, t // tt),
        in_specs=[spec(col_q, kw), spec(col_k, kw), spec(col_v, vw), spec(col_r, vw), spec(col_low, LANES),
                  pl.BlockSpec((LANES, kw), lambda i, j: (0, 0)),
                  pl.BlockSpec((1, kw), lambda i, j: (0, 0)),
                  pl.BlockSpec((1, hv), lambda i, j: (0, 0)),
                  state],
        out_specs=[pl.BlockSpec((1, tt, vw), lambda i, j: (i, j, 0)), state],
        out_shape=[jax.ShapeDtypeStruct((b, t, vw), BF16),
                   jax.ShapeDtypeStruct((b, HEADS, hv, GLA_HK_PAD), F32)],
        scratch_shapes=[pltpu.VMEM((HEADS, hv, GLA_HK_PAD), F32)],
        compiler_params=_params(("parallel", "arbitrary")),
        name="gla",
    )(h3, h3, h3, h3, h3, wgate, bgate, nrm.reshape(1, hv), s0t)


def _pad_last(a, width):
    return jnp.pad(a, ((0, 0),) * (a.ndim - 1) + ((0, width - a.shape[-1]),))


def _pad_heads(a, hk):
    lead = a.shape[:-1]
    return _pad_last(a.reshape(lead + (HEADS, hk)), GLA_HK_PAD).reshape(lead + (HEADS * GLA_HK_PAD,))


def _layout(d_model):
    pool_w = 3 * d_model // 8
    attn_w = N_GROUPS * HEADS * HEAD_DIM
    dv = 3 * d_model // 8
    dk = dv // 2
    sections = (pool_w, attn_w, attn_w, attn_w, dk, dk, dv, GLA_RANK, dv, 3 * d_model)
    src = {}
    acc = 0
    for name, w in zip(("u", "aq", "ak", "av", "gq", "gk", "gv", "glow", "gr", "gates"), sections):
        src[name] = (acc, w)
        acc += w
    order = ("u", "aq", "ak", "av", "gv", "gr", "gq", "gk", "glow")
    widths = {"u": pool_w, "aq": attn_w, "ak": attn_w, "av": attn_w, "gv": dv, "gr": dv,
              "gq": HEADS * GLA_HK_PAD, "gk": HEADS * GLA_HK_PAD, "glow": COL_TILE}
    dst = {}
    acc = 0
    for name in order:
        dst[name] = acc
        acc += widths[name]
    return src, dst, widths, order, dk // HEADS, dv // HEADS


def _prep_weights(d_model, ffn_w_gate, ffn_w_up, ffn_w_down, w_in, pool_w, gla_w_gate, gla_b_gate,
                  w_br_pool, w_br_attn, w_br_gla, w_out):
    src, dst, widths, order, hk, hv = _layout(d_model)
    parts = []
    for name in order + ("gates",):
        off, w = src[name]
        blk = w_in[:, :, off:off + w]
        if name in ("gq", "gk"):
            blk = _pad_heads(blk, hk)
        elif name == "glow":
            blk = _pad_last(blk, widths[name])
        parts.append(blk)
    wgate = _pad_heads(jnp.pad(gla_w_gate, ((0, 0), (0, LANES - GLA_RANK), (0, 0))), hk)
    bgate = _pad_heads(gla_b_gate, hk)[:, None, :]
    return dict(
        wg=ffn_w_gate, wu=ffn_w_up,
        wd=ffn_w_down.astype(BF16),
        w_in=jnp.concatenate(parts, axis=2).astype(BF16),
        gate_col=sum(widths[n] for n in order),
        pool_w=pool_w.astype(BF16),
        gla_wgate=wgate.astype(BF16),
        gla_bgate=bgate,
        w_br_pool=w_br_pool.astype(BF16),
        w_br_attn=w_br_attn.astype(BF16),
        w_br_gla=w_br_gla.astype(BF16),
        w_out=w_out.astype(BF16),
    )


def _ffn_block(xp, xs, pw, l, i, gain, bias):
    hid_p, hid_s = _ffn_hidden(xp[1], xs[1], pw["wg"], pw["wu"], (l, i))
    of, ob, ofs, obs = _matmul_res_ln(hid_p, hid_s, pw["wd"], (l, i), xp[0], xs[0], gain, bias, 0.5, nkh=2)
    return (of, ob), (ofs, obs)


def _kv_rows(h3, dst, g, rows):
    gw = HEADS * HEAD_DIM
    b = h3.shape[0]
    k = h3[:, rows, dst["ak"] + g * gw: dst["ak"] + (g + 1) * gw]
    v = h3[:, rows, dst["av"] + g * gw: dst["av"] + (g + 1) * gw]
    return jnp.stack([k, v], axis=2).reshape(b, k.shape[1], 2, HEADS, HEAD_DIM)


def _mixers(h, l, bsz, t_new, decode, pw, lw, caches):
    m = h.shape[0]
    src, dst, widths, order, hk, hv = _layout(lw["ln_gain"].shape[-1])
    h3 = h.reshape(bsz, t_new, -1)
    pool_c = widths["u"]
    u = h3[:, :, dst["u"]:dst["u"] + pool_c]
    if decode:
        pool_buf, kv_bufs, gla_s = caches
        t_pad = SUBLANES
        p0 = PAST_LEN
        h3p = jnp.pad(h3, ((0, 0), (0, t_pad - t_new), (0, 0)))
        hist = pool_buf
    else:
        t_pad = t_new
        p0 = 0
        h3p = h3
        hist = jnp.zeros((bsz, POOL_KEEP, pool_c), F32)
        gla_s = jnp.zeros((bsz, HEADS, hk, hv), F32)
    hist_ext = jnp.pad(hist, ((0, 0), (POOL_HALO - POOL_KEEP, 0), (0, 0)))
    y_pool = _pool(h3p, dst["u"], hist_ext, pw["pool_w"][l], lw["pool_scale"], p0)[:, :t_new]
    new_pool = jnp.concatenate([hist, u], axis=1)[:, -POOL_KEEP:]
    aw = widths["aq"]
    if decode:
        q, kn, vn = (h3p[:, :, dst[n]:dst[n] + aw] for n in ("aq", "ak", "av"))
        flat = [c.reshape(c.shape[0], bsz, c.shape[2] * 2 * HEADS, HEAD_DIM) for c in kv_bufs]
        y_attn = _attn_decode(q, kn, vn, flat, l, *lw["decode_tabs"])[:, :t_new]
        new_kv = [_kv_rows(h3, dst, g, slice(None)) for g in range(N_GROUPS)]
    else:
        y_attn, new_kv = _attn_prompt(h3, lw["band_tabs"], dst["aq"], dst["ak"], dst["av"])
    s0t = jnp.pad(jnp.swapaxes(gla_s.astype(F32), -1, -2), ((0, 0), (0, 0), (0, 0), (0, GLA_HK_PAD - hk)))
    chunk = SUBLANES if decode else math.gcd(t_new, GLA_CHUNK)
    y_gla, st = _gla(h3p, pw["gla_wgate"][l], pw["gla_bgate"][l], lw["gla_norm"], s0t,
                     (dst["gq"], dst["gk"], dst["gv"], dst["gr"], dst["glow"]), chunk,
                     t_new if decode else chunk, hk)
    y_gla = y_gla[:, :t_new]
    new_s = jnp.swapaxes(st[..., :hk], -1, -2)
    ys = [y.reshape(m, -1) for y in (y_pool, y_attn, y_gla)]
    return ys, new_pool, new_kv, new_s


def _layer(xp, xs, l, shape_p, shape_s, pw, lw, caches):
    xp, xs = _ffn_block(xp, xs, pw, l, 0, lw["ln_gain"][0], lw["ln_bias"][0])
    h_p, h_s = _matmul(xp[1], xs[1], pw["w_in"], (l,))
    ys_p, *new_p = _mixers(h_p, l, *shape_p, False, pw, lw, None)
    ys_s, *new_s = _mixers(h_s, l, *shape_s, True, pw, lw, caches)
    merged_p, merged_s = _merge(ys_p, ys_s, [pw["w_br_pool"], pw["w_br_attn"], pw["w_br_gla"]], (l,),
                                h_p, h_s, pw["gate_col"])
    of, ob, ofs, obs = _matmul_res_ln(merged_p, merged_s, pw["w_out"], (l,), xp[0], xs[0],
                                      lw["ln_gain"][1], lw["ln_bias"][1], 1.0, nkh=1, tn=2 * COL_TILE)
    xp, xs = _ffn_block((of, ob), (ofs, obs), pw, l, 1, lw["ln_gain"][2], lw["ln_bias"][2])
    return xp, xs, new_p, new_s


def kernel(x_prompt, x_sample, cache_pool, cache_kv_w128, cache_kv_w512, cache_kv_w2048, state_gla, rel_bias, ln_gain, ln_bias, ffn_w_gate, ffn_w_up, ffn_w_down, w_in, pool_w, pool_scale, gla_w_gate, gla_b_gate, gla_norm, w_br_pool, w_br_attn, w_br_gla, w_out):
    d_model = x_prompt.shape[-1]
    pw = _prep_weights(d_model, ffn_w_gate, ffn_w_up, ffn_w_down, w_in, pool_w, gla_w_gate,
                       gla_b_gate, w_br_pool, w_br_attn, w_br_gla, w_out)
    band_tabs = _band_tables(rel_bias)
    decode_tabs = _decode_tables(rel_bias, x_sample.shape[1], SUBLANES)
    kv_caches = [cache_kv_w128, cache_kv_w512, cache_kv_w2048]
    shape_p = x_prompt.shape[:2]
    shape_s = x_sample.shape[:2]
    xp = x_prompt.reshape(-1, d_model)
    xs = x_sample.reshape(-1, d_model)
    xp = (xp, xp.astype(BF16))
    xs = (xs, xs.astype(BF16))
    outs_p, outs_s = [], []
    for l in range(DEPTH):
        lw = dict(ln_gain=ln_gain[l], ln_bias=ln_bias[l], pool_scale=pool_scale[l], gla_norm=gla_norm[l],
                  band_tabs=band_tabs, decode_tabs=decode_tabs)
        xp, xs, new_p, new_s = _layer(xp, xs, l, shape_p, shape_s, pw, lw,
                                      (cache_pool[l], kv_caches, state_gla[l]))
        outs_p.append(new_p)
        outs_s.append(new_s)

    def stacked(outs):
        pools = jnp.stack([o[0] for o in outs])
        kvs = [jnp.stack([o[1][g] for o in outs]) for g in range(N_GROUPS)]
        return pools, kvs, jnp.stack([o[2] for o in outs])

    pool_p, kv_p, gla_p = stacked(outs_p)
    pool_s, rows_s, gla_s = stacked(outs_s)
    t_s = shape_s[1]
    def shift_in(c, r):
        cfg = [(0, 0, 0)] * c.ndim
        cfg[2] = (-t_s, t_s, 0)
        shifted = lax.pad(c, jnp.zeros((), c.dtype), cfg)
        return lax.dynamic_update_slice_in_dim(shifted, r.astype(c.dtype), c.shape[2] - t_s, axis=2)

    kv_s = [shift_in(c, r) for c, r in zip(kv_caches, rows_s)]
    return (xp[0].reshape(x_prompt.shape), xs[0].reshape(x_sample.shape),
            pool_p, kv_p[0], kv_p[1], kv_p[2], gla_p,
            pool_s, kv_s[0], kv_s[1], kv_s[2], gla_s)
```
